```python
import math
import jax, jax.numpy as jnp
from jax import lax
import numpy as np

D_MODEL = 2048
BATCH = 1
SEQ = 8192
DEPTH = 4

GRID_W = 64
CTX_LEN = 256
N_MIXERS = 3
HEAD_DIM = 128
GQA_HEADS = D_MODEL // HEAD_DIM
GQA_KV_HEADS = GQA_HEADS // 4
DIFF_HEAD_DIM = 128
DIFF_HEADS = D_MODEL // (2 * DIFF_HEAD_DIM)
FNET_GROUPS = 4
FNET_GROUP_DIM = D_MODEL // FNET_GROUPS
D_FF = 256 * math.ceil(8 * D_MODEL / 3 / 256)
CONV_WIDTH = 3
Q_BLOCK = 128
ROPE_THETA = 10000.0
EPS = 1e-6
ALPHA = (2 * DEPTH) ** 0.25
BETA = (8 * DEPTH) ** -0.25

kernel_name = 'hybrid_interleaved_dit_block'


def layer_norm(x, g, b):
    xf = x.astype(jnp.float32)
    xc = xf - jnp.mean(xf, axis=-1, keepdims=True)
    var = jnp.mean(xc * xc, axis=-1, keepdims=True)
    return (xc * lax.rsqrt(var + EPS) * g.astype(jnp.float32) + b.astype(jnp.float32)).astype(x.dtype)


def rms_norm(x, g):
    xf = x.astype(jnp.float32)
    ms = jnp.mean(xf * xf, axis=-1, keepdims=True)
    return (xf * lax.rsqrt(ms + EPS) * g.astype(jnp.float32)).astype(x.dtype)


def modulate(x, shift, scale):
    return x * (1 + scale) + shift


def axial_rope_cos_sin(row, col, head_dim):
    quarter = head_dim // 4
    inv_freq = ROPE_THETA ** (-jnp.arange(quarter, dtype=jnp.float32) / quarter)
    ang = jnp.concatenate([row.astype(jnp.float32)[:, None] * inv_freq[None, :],
                           col.astype(jnp.float32)[:, None] * inv_freq[None, :]], axis=-1)
    return jnp.cos(ang), jnp.sin(ang)


def apply_axial_rope(x, cos, sin):
    B, S, H, hd = x.shape
    q = hd // 4
    xf = x.astype(jnp.float32).reshape(B, S, H, 2, 2, q)
    x1, x2 = xf[..., 0, :], xf[..., 1, :]
    c = cos.reshape(1, S, 1, 2, q)
    s = sin.reshape(1, S, 1, 2, q)
    out = jnp.stack([x1 * c - x2 * s, x1 * s + x2 * c], axis=-2)
    return out.reshape(B, S, H, hd).astype(x.dtype)


def sweep_query_blocks(block_fn, q):
    B, S = q.shape[:2]
    nblk = S // Q_BLOCK
    qb = jnp.moveaxis(q.reshape(B, nblk, Q_BLOCK, *q.shape[2:]), 1, 0)
    out = lax.map(block_fn, qb)
    return jnp.moveaxis(out, 0, 1).reshape(B, S, *out.shape[3:])


def gqa_attend(q, k, v):
    B, Q = q.shape[:2]
    qg = q.reshape(B, Q, GQA_KV_HEADS, GQA_HEADS // GQA_KV_HEADS, HEAD_DIM)
    s = jnp.einsum('bqkgd,blkd->bkgql', qg, k).astype(jnp.float32) * (HEAD_DIM ** -0.5)
    p = jax.nn.softmax(s, axis=-1).astype(v.dtype)
    o = jnp.einsum('bkgql,blkd->bqkgd', p, v)
    return o.reshape(B, Q, GQA_HEADS * HEAD_DIM)


def gqa_mixer(h_lat, h_ctx, w_qkv, q_norm, k_norm, w_o, cos, sin, need_ctx):
    def project(h):
        B, L, _ = h.shape
        q, k, v = jnp.split(h @ w_qkv, [GQA_HEADS * HEAD_DIM, (GQA_HEADS + GQA_KV_HEADS) * HEAD_DIM], axis=-1)
        q = rms_norm(q.reshape(B, L, GQA_HEADS, HEAD_DIM), q_norm)
        k = rms_norm(k.reshape(B, L, GQA_KV_HEADS, HEAD_DIM), k_norm)
        return q, k, v.reshape(B, L, GQA_KV_HEADS, HEAD_DIM)

    q_l, k_l, v_l = project(h_lat)
    q_c, k_c, v_c = project(h_ctx)
    q_l = apply_axial_rope(q_l, cos, sin)
    k_l = apply_axial_rope(k_l, cos, sin)
    k_all = jnp.concatenate([k_l, k_c], axis=1)
    v_all = jnp.concatenate([v_l, v_c], axis=1)
    o_lat = sweep_query_blocks(lambda qb: gqa_attend(qb, k_all, v_all), q_l) @ w_o
    o_ctx = gqa_attend(q_c, k_c, v_c) @ w_o if need_ctx else None
    return o_lat, o_ctx


def diff_attend(q, k, v, lam, subln, lambda_init):
    B, Q = q.shape[:2]
    L = k.shape[1]
    q = q.reshape(B, Q, DIFF_HEADS, 2, DIFF_HEAD_DIM)
    k = k.reshape(B, L, DIFF_HEADS, 2, DIFF_HEAD_DIM)
    s = jnp.einsum('bqhcd,blhcd->bhcql', q, k).astype(jnp.float32) * (DIFF_HEAD_DIM ** -0.5)
    p = jax.nn.softmax(s, axis=-1)
    a = p[:, :, 0] - lam * p[:, :, 1]
    o = jnp.einsum('bhql,blhe->bqhe', a.astype(v.dtype), v)
    o = rms_norm(o, subln) * (1.0 - lambda_init)
    return o.reshape(B, Q, D_MODEL)


def diff_mixer(h_lat, h_ctx, w_qkv, lam_params, subln, w_o, cos, sin, lambda_init, need_ctx):
    def project(h):
        B, L, _ = h.shape
        q, k, v = jnp.split(h @ w_qkv, 3, axis=-1)
        return (q.reshape(B, L, 2 * DIFF_HEADS, DIFF_HEAD_DIM),
                k.reshape(B, L, 2 * DIFF_HEADS, DIFF_HEAD_DIM),
                v.reshape(B, L, DIFF_HEADS, 2 * DIFF_HEAD_DIM))

    lp = lam_params.astype(jnp.float32)
    lam = jnp.exp(jnp.sum(lp[0] * lp[1])) - jnp.exp(jnp.sum(lp[2] * lp[3])) + lambda_init
    q_l, k_l, v_l = project(h_lat)
    q_c, k_c, v_c = project(h_ctx)
    q_l = apply_axial_rope(q_l, cos, sin)
    k_l = apply_axial_rope(k_l, cos, sin)
    k_all = jnp.concatenate([k_l, k_c], axis=1)
    v_all = jnp.concatenate([v_l, v_c], axis=1)
    o_lat = sweep_query_blocks(lambda qb: diff_attend(qb, k_all, v_all, lam, subln, lambda_init), q_l) @ w_o
    o_ctx = diff_attend(q_c, k_c, v_c, lam, subln, lambda_init) @ w_o if need_ctx else None
    return o_lat, o_ctx


def fourier_mixer(h, w, b):
    B, L, _ = h.shape
    hg = h.astype(jnp.float32).reshape(B, L, FNET_GROUPS, FNET_GROUP_DIM)
    y = jnp.fft.fftn(hg, axes=(1, 3), norm='ortho').real
    return y.reshape(B, L, D_MODEL).astype(h.dtype) @ w + b


def conv_ffn(h, w_up, conv_w, conv_b, w_down):
    u = h @ w_up
    L = u.shape[1]
    pad = CONV_WIDTH // 2
    up = jnp.pad(u, ((0, 0), (pad, pad), (0, 0)))
    u = sum(conv_w[t] * up[:, t:t + L] for t in range(CONV_WIDTH)) + conv_b
    val, gate = jnp.split(u, 2, axis=-1)
    return (jax.nn.silu(gate) * val) @ w_down


def setup_inputs(seed: int = 0) -> dict:
    key = jax.random.key(seed)
    ks = iter(jax.random.split(key, 32))
    n_a = len(range(0, DEPTH, N_MIXERS))
    n_b = len(range(1, DEPTH, N_MIXERS))
    n_c = len(range(2, DEPTH, N_MIXERS))
    D = D_MODEL

    def normal(shape, scale=1.0):
        return jax.random.normal(next(ks), shape, jnp.float32) * scale

    def dense(shape, fan_in, gain=1.0):
        return normal(shape, gain * fan_in ** -0.5)

    def near_one(shape):
        return 1.0 + normal(shape, 0.02)

    return {
        'x': normal((BATCH, SEQ, D)),
        'c': normal((BATCH, D)),
        'ctx': normal((BATCH, CTX_LEN, D)),
        'c_ctx': normal((D,)),
        'ada_w': dense((DEPTH, D, 6 * D), D, 0.5),
        'ada_b': normal((DEPTH, 6 * D), 0.02),
        'ln_g': near_one((DEPTH, 2, D)),
        'ln_b': normal((DEPTH, 2, D), 0.02),
        'gqa_w_qkv': dense((n_a, D, (GQA_HEADS + 2 * GQA_KV_HEADS) * HEAD_DIM), D),
        'gqa_q_norm': near_one((n_a, HEAD_DIM)),
        'gqa_k_norm': near_one((n_a, HEAD_DIM)),
        'gqa_w_o': dense((n_a, D, D), D, BETA),
        'diff_w_qkv': dense((n_b, D, 3 * D), D),
        'diff_lambda': normal((n_b, 4, DIFF_HEAD_DIM), 0.1),
        'diff_subln': near_one((n_b, 2 * DIFF_HEAD_DIM)),
        'diff_w_o': dense((n_b, D, D), D, BETA),
        'fnet_w': dense((n_c, D, D), D, BETA),
        'fnet_b': normal((n_c, D), 0.02),
        'ffn_w_up': dense((DEPTH, D, 2 * D_FF), D),
        'ffn_conv_w': dense((DEPTH, CONV_WIDTH, 2 * D_FF), CONV_WIDTH),
        'ffn_conv_b': normal((DEPTH, 2 * D_FF), 0.02),
        'ffn_w_down': dense((DEPTH, D_FF, D), D_FF, BETA),
    }


def reference(x, c, ctx, c_ctx, ada_w, ada_b, ln_g, ln_b, gqa_w_qkv, gqa_q_norm, gqa_k_norm, gqa_w_o,
              diff_w_qkv, diff_lambda, diff_subln, diff_w_o, fnet_w, fnet_b,
              ffn_w_up, ffn_conv_w, ffn_conv_b, ffn_w_down):
    B, S, _ = x.shape
    ROWS = S // GRID_W
    row = jnp.repeat(jnp.arange(ROWS, dtype=jnp.int32), GRID_W)
    col = jnp.tile(jnp.arange(GRID_W, dtype=jnp.int32), ROWS)
    cos, sin = axial_rope_cos_sin(row, col, HEAD_DIM)

    x_l, x_c = x, ctx
    silu_c = jax.nn.silu(c)
    silu_cc = jax.nn.silu(c_ctx)
    counts = [0, 0, 0]
    for i in range(DEPTH):
        last = i == DEPTH - 1
        mod_l = jnp.split((silu_c @ ada_w[i] + ada_b[i])[:, None, :], 6, axis=-1)
        mod_c = jnp.split(silu_cc @ ada_w[i] + ada_b[i], 6, axis=-1)
        a_l = modulate(x_l, mod_l[0], mod_l[1])
        a_c = modulate(x_c, mod_c[0], mod_c[1])
        kind = i % N_MIXERS
        j = counts[kind]
        counts[kind] += 1
        if kind == 0:
            o_l, o_c = gqa_mixer(a_l, a_c, gqa_w_qkv[j], gqa_q_norm[j], gqa_k_norm[j], gqa_w_o[j],
                                 cos, sin, not last)
        elif kind == 1:
            lambda_init = 0.8 - 0.6 * math.exp(-0.3 * i)
            o_l, o_c = diff_mixer(a_l, a_c, diff_w_qkv[j], diff_lambda[j], diff_subln[j], diff_w_o[j],
                                  cos, sin, lambda_init, not last)
        else:
            o_l = fourier_mixer(a_l, fnet_w[j], fnet_b[j])
            o_c = None if last else fourier_mixer(a_c, fnet_w[j], fnet_b[j])
        x_l = layer_norm(ALPHA * x_l + mod_l[2] * o_l, ln_g[i, 0], ln_b[i, 0])
        f_l = conv_ffn(modulate(x_l, mod_l[3], mod_l[4]), ffn_w_up[i], ffn_conv_w[i], ffn_conv_b[i], ffn_w_down[i])
        x_l = layer_norm(ALPHA * x_l + mod_l[5] * f_l, ln_g[i, 1], ln_b[i, 1])
        if not last:
            x_c = layer_norm(ALPHA * x_c + mod_c[2] * o_c, ln_g[i, 0], ln_b[i, 0])
            f_c = conv_ffn(modulate(x_c, mod_c[3], mod_c[4]), ffn_w_up[i], ffn_conv_w[i], ffn_conv_b[i], ffn_w_down[i])
            x_c = layer_norm(ALPHA * x_c + mod_c[5] * f_c, ln_g[i, 1], ln_b[i, 1])
    return x_l
```

```python
import functools
import math

import numpy as np
import jax
import jax.numpy as jnp
from jax import lax
from jax.experimental import pallas as pl
from jax.experimental.pallas import tpu as pltpu

HEAD_DIM = 128
GQA_GROUP = 4
GRID_W = 64
N_MIXERS = 3
FNET_GROUPS = 4
ROPE_THETA = 10000.0
EPS = 1e-6
ROW_GROUP = 256
FFT_MINOR = 128
BF16_SUBLANES = 16
V7X_VMEM_LIMIT_BYTES = 56 * 1024 * 1024

_F32 = jnp.float32
_BF16 = jnp.bfloat16


def _params(n_grid_axes):
    return pltpu.CompilerParams(dimension_semantics=("arbitrary",) * n_grid_axes,
                                vmem_limit_bytes=V7X_VMEM_LIMIT_BYTES)


def _largest_tile(n, unit, cap):
    best = None
    for t in range(unit, min(n, cap) + 1, unit):
        if n % t == 0:
            best = t
    assert best is not None, (n, unit, cap)
    return best


def _silu(x):
    return x * (1.0 / (1.0 + jnp.exp(-x)))


def _ada_kernel(cv_ref, w_ref, b_ref, o_ref):
    s = _silu(cv_ref[...]).astype(_BF16)
    acc = jnp.dot(s, w_ref[...].astype(_BF16), preferred_element_type=_F32)
    o_ref[...] = acc + b_ref[...]


def _ada_mods(cvec, ada_w, ada_b):
    depth, d, n = ada_w.shape
    tn = _largest_tile(n, 128, 1024)
    return pl.pallas_call(
        _ada_kernel,
        grid=(depth, n // tn),
        in_specs=[pl.BlockSpec((8, d), lambda l, j: (0, 0)),
                  pl.BlockSpec((None, d, tn), lambda l, j: (l, 0, j)),
                  pl.BlockSpec((None, 1, tn), lambda l, j: (l, 0, j))],
        out_specs=pl.BlockSpec((None, 8, tn), lambda l, j: (l, 0, j)),
        out_shape=jax.ShapeDtypeStruct((depth, 8, n), _F32),
        compiler_params=_params(2),
    )(cvec, ada_w, ada_b.reshape(depth, 1, n))


def _ln_mod_kernel(*refs, do_ln, emit_h, n_sub, n_lat_groups):
    it = iter(refs)
    z_ref = next(it)
    g_ref = next(it) if do_ln else None
    b_ref = next(it) if do_ln else None
    sh_ref = next(it) if emit_h else None
    sc_ref = next(it) if emit_h else None
    x_ref = next(it) if do_ln else None
    h_ref = next(it) if emit_h else None
    i = pl.program_id(0)
    for s in range(n_sub):
        rows = slice(s * ROW_GROUP, (s + 1) * ROW_GROUP)
        x = z_ref[rows, :]
        if do_ln:
            mu = jnp.mean(x, axis=-1, keepdims=True)
            xc = x - mu
            var = jnp.mean(xc * xc, axis=-1, keepdims=True)
            x = xc * lax.rsqrt(var + EPS) * g_ref[...] + b_ref[...]
            x_ref[rows, :] = x
        if emit_h:
            is_ctx = (i * n_sub + s) >= n_lat_groups
            sh = jnp.where(is_ctx, sh_ref[1:2, :], sh_ref[0:1, :])
            sc = jnp.where(is_ctx, sc_ref[1:2, :], sc_ref[0:1, :])
            h_ref[rows, :] = (x * (1.0 + sc) + sh).astype(_BF16)


def _ln_mod(z, n_rows, seq, ln_g2, ln_b2, ln_idx, mods, mod_layer, shift_idx, *, do_ln, emit_h):
    d = z.shape[1]
    tm = _largest_tile(n_rows, ROW_GROUP, 1024)
    in_specs = [pl.BlockSpec((tm, d), lambda i: (i, 0))]
    args = [z]
    if do_ln:
        in_specs += [pl.BlockSpec((None, 1, d), lambda i: (ln_idx, 0, 0))] * 2
        args += [ln_g2, ln_b2]
    if emit_h:
        in_specs += [pl.BlockSpec((None, 8, d), lambda i: (mod_layer, 0, shift_idx)),
                     pl.BlockSpec((None, 8, d), lambda i: (mod_layer, 0, shift_idx + 1))]
        args += [mods, mods]
    out_specs, out_shape = [], []
    if do_ln:
        out_specs.append(pl.BlockSpec((tm, d), lambda i: (i, 0)))
        out_shape.append(jax.ShapeDtypeStruct((n_rows, d), _F32))
    if emit_h:
        out_specs.append(pl.BlockSpec((tm, d), lambda i: (i, 0)))
        out_shape.append(jax.ShapeDtypeStruct((n_rows, d), _BF16))
    outs = pl.pallas_call(
        functools.partial(_ln_mod_kernel, do_ln=do_ln, emit_h=emit_h, n_sub=tm // ROW_GROUP,
                          n_lat_groups=seq // ROW_GROUP),
        grid=(n_rows // tm,),
        in_specs=in_specs, out_specs=out_specs, out_shape=out_shape,
        compiler_params=_params(1),
    )(*args)
    return outs


def _rope(y, cos, sin_signed):
    lane = lax.broadcasted_iota(jnp.int32, y.shape, 1)
    partner = jnp.where((lane % 64) < 32, pltpu.roll(y, 96, 1), pltpu.roll(y, 32, 1))
    return y * cos + partner * sin_signed


def _head_proj_kernel(*refs, mode, has_norm, out_scale):
    it = iter(refs)
    h_ref, w_ref = next(it), next(it)
    gain_ref = next(it) if has_norm else None
    cos_ref = next(it) if mode != "v" else None
    sin_ref = next(it) if mode != "v" else None
    o_ref, wbf_ref = next(it), next(it)

    @pl.when(pl.program_id(1) == 0)
    def _():
        wbf_ref[...] = w_ref[...].astype(_BF16)

    acc = jnp.dot(h_ref[...], wbf_ref[...], preferred_element_type=_F32)
    if mode == "v":
        o_ref[...] = acc.astype(_BF16)
        return
    cos, sin_signed = cos_ref[...], sin_ref[...]
    for hh in range(acc.shape[1] // HEAD_DIM):
        cols = slice(hh * HEAD_DIM, (hh + 1) * HEAD_DIM)
        y = acc[:, cols]
        if has_norm:
            ms = jnp.mean(y * y, axis=-1, keepdims=True)
            y = y * lax.rsqrt(ms + EPS) * gain_ref[...]
        y = _rope(y, cos, sin_signed)
        if out_scale != 1.0:
            y = y * out_scale
        if mode == "q":
            o_ref[:, cols] = y.astype(_BF16)
        else:
            o_ref[cols, :] = y.T.astype(_BF16)


def _head_proj(h, w, w_layer, col0, n_cols, mode, cos_tab, sin_tab, gain=None, out_scale=1.0):
    rows, d = h.shape
    tm = _largest_tile(rows, ROW_GROUP, 1024)
    tn = _largest_tile(math.gcd(n_cols, col0) if col0 else n_cols, HEAD_DIM, 1024)
    j0 = col0 // tn
    in_specs = [pl.BlockSpec((tm, d), lambda j, i: (i, 0)),
                pl.BlockSpec((None, d, tn), lambda j, i: (w_layer, 0, j0 + j))]
    args = [h, w]
    if gain is not None:
        in_specs.append(pl.BlockSpec((None, 1, HEAD_DIM), lambda j, i: (w_layer, 0, 0)))
        args.append(gain)
    if mode != "v":
        in_specs += [pl.BlockSpec((tm, HEAD_DIM), lambda j, i: (i, 0))] * 2
        args += [cos_tab, sin_tab]
    if mode == "k":
        out_spec = pl.BlockSpec((tn, tm), lambda j, i: (j, i))
        out_shape = jax.ShapeDtypeStruct((n_cols, rows), _BF16)
    else:
        out_spec = pl.BlockSpec((tm, tn), lambda j, i: (i, j))
        out_shape = jax.ShapeDtypeStruct((rows, n_cols), _BF16)
    return pl.pallas_call(
        functools.partial(_head_proj_kernel, mode=mode, has_norm=gain is not None, out_scale=out_scale),
        grid=(n_cols // tn, rows // tm),
        in_specs=in_specs, out_specs=out_spec, out_shape=out_shape,
        scratch_shapes=[pltpu.VMEM((d, tn), _BF16)],
        compiler_params=_params(2),
    )(*args)


def _resid_kernel(*refs, alpha, has_bias, n_sub, n_lat_groups):
    it = iter(refs)
    a_ref, w_ref = next(it), next(it)
    b_ref = next(it) if has_bias else None
    x_ref, g_ref, z_ref, wbf_ref = next(it), next(it), next(it), next(it)
    i = pl.program_id(1)

    @pl.when(i == 0)
    def _():
        wbf_ref[...] = w_ref[...].astype(_BF16)

    acc = jnp.dot(a_ref[...], wbf_ref[...], preferred_element_type=_F32)
    if has_bias:
        acc = acc + b_ref[...]
    for s in range(n_sub):
        rows = slice(s * ROW_GROUP, (s + 1) * ROW_GROUP)
        is_ctx = (i * n_sub + s) >= n_lat_groups
        gate = jnp.where(is_ctx, g_ref[1:2, :], g_ref[0:1, :])
        z_ref[rows, :] = alpha * x_ref[rows, :] + gate * acc[rows, :]


def _resid_proj(a, w, w_layer, bias, xs, mods, mod_layer, gate_idx, seq, alpha):
    rows, k = a.shape
    d = w.shape[2]
    tm = _largest_tile(rows, ROW_GROUP, 1024)
    tn = _largest_tile(d, 128, 1024 if k <= 2048 else 512)
    gate_blk0 = gate_idx * (d // tn)
    in_specs = [pl.BlockSpec((tm, k), lambda j, i: (i, 0)),
                pl.BlockSpec((None, k, tn), lambda j, i: (w_layer, 0, j))]
    args = [a, w]
    if bias is not None:
        in_specs.append(pl.BlockSpec((None, 1, tn), lambda j, i: (w_layer, 0, j)))
        args.append(bias)
    in_specs += [pl.BlockSpec((tm, tn), lambda j, i: (i, j)),
                 pl.BlockSpec((None, 8, tn), lambda j, i: (mod_layer, 0, gate_blk0 + j))]
    args += [xs, mods]
    return pl.pallas_call(
        functools.partial(_resid_kernel, alpha=alpha, has_bias=bias is not None, n_sub=tm // ROW_GROUP,
                          n_lat_groups=seq // ROW_GROUP),
        grid=(d // tn, rows // tm),
        in_specs=in_specs,
        out_specs=pl.BlockSpec((tm, tn), lambda j, i: (i, j)),
        out_shape=jax.ShapeDtypeStruct((rows, d), _F32),
        scratch_shapes=[pltpu.VMEM((k, tn), _BF16)],
        compiler_params=_params(2),
    )(*args)


def _ffn_up_kernel(hp_ref, hm_ref, hn_ref, wv_ref, wg_ref, cwv_ref, cwg_ref, cbv_ref, cbg_ref, o_ref,
                   wv_bf, wg_bf, hs_ref, *, seq, rows):
    i = pl.program_id(1)
    tm = hm_ref.shape[0]
    halo = BF16_SUBLANES

    @pl.when(i == 0)
    def _():
        wv_bf[...] = wv_ref[...].astype(_BF16)
        wg_bf[...] = wg_ref[...].astype(_BF16)

    hs_ref[0:halo, :] = hp_ref[...]
    hs_ref[halo:halo + tm, :] = hm_ref[...]
    hs_ref[halo + tm:, :] = hn_ref[...]
    hs = hs_ref[...]
    r = i * tm + lax.broadcasted_iota(jnp.int32, (tm, 1), 0)
    keep_prev = jnp.where(r == 0, 0.0, 1.0) * jnp.where(r == seq, 0.0, 1.0)
    keep_next = jnp.where(r == seq - 1, 0.0, 1.0) * jnp.where(r == rows - 1, 0.0, 1.0)

    def conv_branch(w_bf, cw_ref, cb_ref):
        u = jnp.dot(hs, w_bf[...], preferred_element_type=_F32)
        cw = cw_ref[...]
        return (cw[1:2, :] * u[halo:halo + tm, :]
                + keep_prev * (cw[0:1, :] * u[halo - 1:halo - 1 + tm, :])
                + keep_next * (cw[2:3, :] * u[halo + 1:halo + 1 + tm, :])
                + cb_ref[...])

    val = conv_branch(wv_bf, cwv_ref, cbv_ref)
    gate = conv_branch(wg_bf, cwg_ref, cbg_ref)
    o_ref[...] = (_silu(gate) * val).astype(_BF16)


def _ffn_up(h, w_up, conv_w, conv_b3, layer, seq):
    rows, d = h.shape
    d_ff = w_up.shape[2] // 2
    tm = _largest_tile(rows, ROW_GROUP, 1024)
    tf = _largest_tile(d_ff, 128, 512)
    nf = d_ff // tf
    halo = BF16_SUBLANES
    per_tile = tm // halo
    last_halo_blk = rows // halo - 1
    return pl.pallas_call(
        functools.partial(_ffn_up_kernel, seq=seq, rows=rows),
        grid=(nf, rows // tm),
        in_specs=[pl.BlockSpec((halo, d), lambda j, i: (jnp.maximum(i * per_tile - 1, 0), 0)),
                  pl.BlockSpec((tm, d), lambda j, i: (i, 0)),
                  pl.BlockSpec((halo, d), lambda j, i: (jnp.minimum((i + 1) * per_tile, last_halo_blk), 0)),
                  pl.BlockSpec((None, d, tf), lambda j, i: (layer, 0, j)),
                  pl.BlockSpec((None, d, tf), lambda j, i: (layer, 0, nf + j)),
                  pl.BlockSpec((None, 3, tf), lambda j, i: (layer, 0, j)),
                  pl.BlockSpec((None, 3, tf), lambda j, i: (layer, 0, nf + j)),
                  pl.BlockSpec((None, 1, tf), lambda j, i: (layer, 0, j)),
                  pl.BlockSpec((None, 1, tf), lambda j, i: (layer, 0, nf + j))],
        out_specs=pl.BlockSpec((tm, tf), lambda j, i: (i, j)),
        out_shape=jax.ShapeDtypeStruct((rows, d_ff), _BF16),
        scratch_shapes=[pltpu.VMEM((d, tf), _BF16), pltpu.VMEM((d, tf), _BF16),
                        pltpu.VMEM((tm + 2 * halo, d), _BF16)],
        compiler_params=_params(2),
    )(h, h, h, w_up, w_up, conv_w, conv_w, conv_b3, conv_b3)


_NEG_BIG = -1e30


def _online_softmax_step(q, k_t, v, m, l, acc):
    s = jnp.dot(q, k_t, preferred_element_type=_F32)
    m_new = jnp.maximum(m, jnp.max(s, axis=1, keepdims=True))
    alpha = jnp.exp(m - m_new)
    p = jnp.exp(s - m_new)
    l = alpha * l + jnp.sum(p, axis=1, keepdims=True)
    acc = alpha * acc + jnp.dot(p.astype(_BF16), v, preferred_element_type=_F32)
    return m_new, l, acc


def _attend(q_list, k_slices, v_ref, n_maps, v_width, *, n_main, seq, ctx, tk):
    tq = q_list[0].shape[0]

    def step(cols, carry):
        v = v_ref[cols, :]
        return tuple(_online_softmax_step(q_list[c], k_slices[c](cols), v, *carry[c]) for c in range(n_maps))

    def body(c, carry):
        return step(pl.ds(pl.multiple_of(c * tk, tk), tk), carry)

    init = tuple((jnp.full((tq, 1), _NEG_BIG, _F32), jnp.zeros((tq, 1), _F32), jnp.zeros((tq, v_width), _F32))
                 for _ in range(n_maps))
    carry = lax.fori_loop(0, n_main, body, init)
    carry = step(slice(seq, seq + ctx), carry)
    return [acc * (1.0 / l) for (_, l, acc) in carry]


def _gqa_attn_kernel(q_ref, kt_ref, v_ref, o_ref, *, seq, ctx, tk):
    i = pl.program_id(1)
    tq = q_ref.shape[0]
    n_main = jnp.where(i < seq // tq, seq // tk, 0)
    q_list = [q_ref[:, g * HEAD_DIM:(g + 1) * HEAD_DIM] for g in range(GQA_GROUP)]
    k_slices = [lambda cols: kt_ref[:, cols]] * GQA_GROUP
    outs = _attend(q_list, k_slices, v_ref, GQA_GROUP, HEAD_DIM, n_main=n_main, seq=seq, ctx=ctx, tk=tk)
    for g in range(GQA_GROUP):
        o_ref[:, g * HEAD_DIM:(g + 1) * HEAD_DIM] = outs[g].astype(_BF16)


def _gqa_attention(q, k_t, v, seq, ctx):
    rows, dq = q.shape
    kvh = k_t.shape[0] // HEAD_DIM
    tq = ROW_GROUP
    qw = GQA_GROUP * HEAD_DIM
    return pl.pallas_call(
        functools.partial(_gqa_attn_kernel, seq=seq, ctx=ctx, tk=_largest_tile(seq, 128, 512)),
        grid=(kvh, rows // tq),
        in_specs=[pl.BlockSpec((tq, qw), lambda g, i: (i, g)),
                  pl.BlockSpec((HEAD_DIM, rows), lambda g, i: (g, 0)),
                  pl.BlockSpec((rows, HEAD_DIM), lambda g, i: (0, g))],
        out_specs=pl.BlockSpec((tq, qw), lambda g, i: (i, g)),
        out_shape=jax.ShapeDtypeStruct((rows, dq), _BF16),
        compiler_params=_params(2),
    )(q, k_t, v)


def _diff_attn_kernel(q_ref, kt_ref, v_ref, lam_ref, sub_ref, o_ref, *, seq, ctx, tk, lambda_init):
    i = pl.program_id(1)
    tq = q_ref.shape[0]
    n_main = jnp.where(i < seq // tq, seq // tk, 0)
    q_list = [q_ref[:, c * HEAD_DIM:(c + 1) * HEAD_DIM] for c in range(2)]
    k_slices = [functools.partial(lambda cols, c: kt_ref[c * HEAD_DIM:(c + 1) * HEAD_DIM, cols], c=c)
                for c in range(2)]
    o0, o1 = _attend(q_list, k_slices, v_ref, 2, 2 * HEAD_DIM, n_main=n_main, seq=seq, ctx=ctx, tk=tk)
    lp = lam_ref[...]
    lam = (jnp.exp(jnp.sum(lp[0:1, :] * lp[1:2, :], axis=1, keepdims=True))
           - jnp.exp(jnp.sum(lp[2:3, :] * lp[3:4, :], axis=1, keepdims=True)) + lambda_init)
    o = o0 - lam * o1
    ms = jnp.mean(o * o, axis=-1, keepdims=True)
    o = o * lax.rsqrt(ms + EPS) * sub_ref[...] * (1.0 - lambda_init)
    o_ref[...] = o.astype(_BF16)


def _diff_attention(q, k_t, v, lam_params, subln, layer_j, seq, ctx, lambda_init):
    rows, d = q.shape
    hw = 2 * HEAD_DIM
    tq = ROW_GROUP
    return pl.pallas_call(
        functools.partial(_diff_attn_kernel, seq=seq, ctx=ctx, tk=_largest_tile(seq, 128, 512),
                          lambda_init=lambda_init),
        grid=(d // hw, rows // tq),
        in_specs=[pl.BlockSpec((tq, hw), lambda h, i: (i, h)),
                  pl.BlockSpec((hw, rows), lambda h, i: (h, 0)),
                  pl.BlockSpec((rows, hw), lambda h, i: (0, h)),
                  pl.BlockSpec((None, 4, HEAD_DIM), lambda h, i: (layer_j, 0, 0)),
                  pl.BlockSpec((None, 1, hw), lambda h, i: (layer_j, 0, 0))],
        out_specs=pl.BlockSpec((tq, hw), lambda h, i: (i, h)),
        out_shape=jax.ShapeDtypeStruct((rows, d), _BF16),
        compiler_params=_params(2),
    )(q, k_t, v, lam_params, subln)


def _dft_cos_sin(n):
    kt = np.outer(np.arange(n), np.arange(n)) % n
    ang = 2.0 * np.pi * kt / n
    return np.cos(ang), np.sin(ang)


def _channel_dft(z, cs_ref, scale, store):
    p = z.shape[0] // 2
    gd = cs_ref.shape[1]
    zb = z.astype(_BF16)
    for g in range(z.shape[1] // gd):
        cols = slice(g * gd, (g + 1) * gd)
        zc = jnp.concatenate([zb[:p, cols], zb[p:, cols]], axis=1)
        y = jnp.dot(zc, cs_ref[...], preferred_element_type=_F32) * scale
        store(cols, y.astype(_BF16))


def _fft_stage1_kernel(a_ref, f1_ref, tw_ref, o_ref, *, d):
    n1 = a_ref.shape[0]
    for tt in range(a_ref.shape[1] // d):
        cols = slice(tt * d, (tt + 1) * d)
        a = jnp.dot(f1_ref[...], a_ref[:, cols], preferred_element_type=_F32)
        ar, ai = a[:n1, :], a[n1:, :]
        twr, twi = tw_ref[0, :, tt:tt + 1], tw_ref[1, :, tt:tt + 1]
        o_ref[0, :, cols] = (ar * twr - ai * twi).astype(_BF16)
        o_ref[1, :, cols] = (ar * twi + ai * twr).astype(_BF16)


def _fft_stage2_kernel(b_ref, g2_ref, cs_ref, y_ref, *, scale):
    d = b_ref.shape[3]
    for kk in range(b_ref.shape[1]):
        bcat = jnp.concatenate([b_ref[0, kk], b_ref[1, kk]], axis=0)
        z = jnp.dot(g2_ref[...], bcat, preferred_element_type=_F32)

        def store(cols, y, kk=kk):
            y_ref[:, kk * d + cols.start:kk * d + cols.stop] = y

        _channel_dft(z, cs_ref, scale, store)


def _ctx_dft_kernel(a_ref, fc_ref, cs_ref, y_ref, *, scale):
    z = jnp.dot(fc_ref[...], a_ref[...], preferred_element_type=_F32)

    def store(cols, y):
        y_ref[:, cols] = y

    _channel_dft(z, cs_ref, scale, store)


def _fourier_mix(h, seq, ctx):
    rows, d = h.shape
    gd = d // FNET_GROUPS
    n2 = FFT_MINOR
    n1 = seq // n2
    g2 = 16
    kb = 2
    assert seq % n2 == 0 and rows % n2 == 0 and n2 % g2 == 0 and n1 % kb == 0

    c1, s1 = _dft_cos_sin(n1)
    f1 = jnp.asarray(np.concatenate([c1, -s1], axis=0), _BF16)
    ang = 2.0 * np.pi * np.outer(np.arange(n1), np.arange(n2)) / seq
    tw = np.stack([np.cos(ang), -np.sin(ang)])
    tw = jnp.asarray(tw.reshape(2, n1, n2 // g2, g2).transpose(2, 0, 1, 3), _F32)
    c2, s2 = _dft_cos_sin(n2)
    g2m = jnp.asarray(np.block([[c2, s2], [-s2, c2]]), _BF16)
    cc, sc = _dft_cos_sin(gd)
    cs = jnp.asarray(np.concatenate([cc, sc], axis=0), _BF16)
    cx, sx = _dft_cos_sin(ctx)
    fc = jnp.asarray(np.concatenate([cx, -sx], axis=0), _BF16)

    b = pl.pallas_call(
        functools.partial(_fft_stage1_kernel, d=d),
        grid=(n2 // g2,),
        in_specs=[pl.BlockSpec((n1, g2 * d), lambda c: (0, c)),
                  pl.BlockSpec((2 * n1, n1), lambda c: (0, 0)),
                  pl.BlockSpec((None, 2, n1, g2), lambda c: (c, 0, 0, 0))],
        out_specs=pl.BlockSpec((2, n1, g2 * d), lambda c: (0, 0, c)),
        out_shape=jax.ShapeDtypeStruct((2, n1, n2 * d), _BF16),
        compiler_params=_params(1),
    )(h.reshape(rows // n2, n2 * d), f1, tw)

    y_lat = pl.pallas_call(
        functools.partial(_fft_stage2_kernel, scale=1.0 / math.sqrt(seq * gd)),
        grid=(n1 // kb,),
        in_specs=[pl.BlockSpec((2, kb, n2, d), lambda k: (0, k, 0, 0)),
                  pl.BlockSpec((2 * n2, 2 * n2), lambda k: (0, 0)),
                  pl.BlockSpec((2 * gd, gd), lambda k: (0, 0))],
        out_specs=pl.BlockSpec((n2, kb * d), lambda k: (0, k)),
        out_shape=jax.ShapeDtypeStruct((n2, n1 * d), _BF16),
        compiler_params=_params(1),
    )(b.reshape(2, n1, n2, d), g2m, cs)

    y_ctx = pl.pallas_call(
        functools.partial(_ctx_dft_kernel, scale=1.0 / math.sqrt(ctx * gd)),
        grid=(1,),
        in_specs=[pl.BlockSpec((ctx, d), lambda c: (seq // ctx, 0)),
                  pl.BlockSpec((2 * ctx, ctx), lambda c: (0, 0)),
                  pl.BlockSpec((2 * gd, gd), lambda c: (0, 0))],
        out_specs=pl.BlockSpec((ctx, d), lambda c: (0, 0)),
        out_shape=jax.ShapeDtypeStruct((ctx, d), _BF16),
        compiler_params=_params(1),
    )(h, fc, cs)
    return jnp.concatenate([y_lat.reshape(seq, d), y_ctx], axis=0)


def _rope_tables(seq, ctx):
    quarter = HEAD_DIM // 4
    t = np.arange(seq)
    inv_freq = ROPE_THETA ** (-np.arange(quarter, dtype=np.float64) / quarter)
    ang_r = (t // GRID_W)[:, None] * inv_freq[None, :]
    ang_c = (t % GRID_W)[:, None] * inv_freq[None, :]
    cos = np.concatenate([np.cos(ang_r)] * 2 + [np.cos(ang_c)] * 2, axis=1)
    sin = np.concatenate([-np.sin(ang_r), np.sin(ang_r), -np.sin(ang_c), np.sin(ang_c)], axis=1)
    cos = np.concatenate([cos, np.ones((ctx, HEAD_DIM))], axis=0)
    sin = np.concatenate([sin, np.zeros((ctx, HEAD_DIM))], axis=0)
    return jnp.asarray(cos, _F32), jnp.asarray(sin, _F32)


def kernel(x, c, ctx, c_ctx, ada_w, ada_b, ln_g, ln_b, gqa_w_qkv, gqa_q_norm, gqa_k_norm, gqa_w_o,
           diff_w_qkv, diff_lambda, diff_subln, diff_w_o, fnet_w, fnet_b,
           ffn_w_up, ffn_conv_w, ffn_conv_b, ffn_w_down):
    batch, seq, d = x.shape
    n_ctx = ctx.shape[1]
    depth = ada_w.shape[0]
    assert batch == 1 and seq % ROW_GROUP == 0 and n_ctx % ROW_GROUP == 0 and seq % GRID_W == 0
    alpha = (2 * depth) ** 0.25
    attn_scale = HEAD_DIM ** -0.5

    cvec = jnp.concatenate([c, c_ctx[None, :], jnp.zeros((6, d), _F32)], axis=0)
    mods = _ada_mods(cvec, ada_w, ada_b)
    xs = jnp.concatenate([x[0], ctx[0]], axis=0)
    rows = seq + n_ctx
    cos_tab, sin_tab = _rope_tables(seq, n_ctx)
    ln_g2 = ln_g.reshape(depth * 2, 1, d)
    ln_b2 = ln_b.reshape(depth * 2, 1, d)
    conv_b3 = ffn_conv_b.reshape(depth, 1, -1)
    gqa_q_norm3 = gqa_q_norm.reshape(-1, 1, HEAD_DIM)
    gqa_k_norm3 = gqa_k_norm.reshape(-1, 1, HEAD_DIM)
    diff_subln3 = diff_subln.reshape(-1, 1, 2 * HEAD_DIM)
    fnet_b3 = fnet_b.reshape(-1, 1, d)

    (h,) = _ln_mod(xs, rows, seq, None, None, 0, mods, 0, 0, do_ln=False, emit_h=True)
    counts = [0, 0, 0]
    for i in range(depth):
        last = i == depth - 1
        kind = i % N_MIXERS
        j = counts[kind]
        counts[kind] += 1
        if kind == 0:
            n_q = d
            n_kv = d // GQA_GROUP
            q = _head_proj(h, gqa_w_qkv, j, 0, n_q, "q", cos_tab, sin_tab, gqa_q_norm3, attn_scale)
            k_t = _head_proj(h, gqa_w_qkv, j, n_q, n_kv, "k", cos_tab, sin_tab, gqa_k_norm3)
            v = _head_proj(h, gqa_w_qkv, j, n_q + n_kv, n_kv, "v", cos_tab, sin_tab)
            o = _gqa_attention(q, k_t, v, seq, n_ctx)
            z = _resid_proj(o, gqa_w_o, j, None, xs, mods, i, 2, seq, alpha)
        elif kind == 1:
            lambda_init = 0.8 - 0.6 * math.exp(-0.3 * i)
            q = _head_proj(h, diff_w_qkv, j, 0, d, "q", cos_tab, sin_tab, None, attn_scale)
            k_t = _head_proj(h, diff_w_qkv, j, d, d, "k", cos_tab, sin_tab)
            v = _head_proj(h, diff_w_qkv, j, 2 * d, d, "v", cos_tab, sin_tab)
            o = _diff_attention(q, k_t, v, diff_lambda, diff_subln3, j, seq, n_ctx, lambda_init)
            z = _resid_proj(o, diff_w_o, j, None, xs, mods, i, 2, seq, alpha)
        else:
            y = _fourier_mix(h, seq, n_ctx)
            z = _resid_proj(y, fnet_w, j, fnet_b3, xs, mods, i, 2, seq, alpha)
        xs, h = _ln_mod(z, rows, seq, ln_g2, ln_b2, 2 * i, mods, i, 3, do_ln=True, emit_h=True)
        a = _ffn_up(h, ffn_w_up, ffn_conv_w, conv_b3, i, seq)
        z = _resid_proj(a, ffn_w_down, i, None, xs, mods, i, 5, seq, alpha)
        if last:
            (xs,) = _ln_mod(z, seq, seq, ln_g2, ln_b2, 2 * i + 1, mods, 0, 0, do_ln=True, emit_h=False)
        else:
            xs, h = _ln_mod(z, rows, seq, ln_g2, ln_b2, 2 * i + 1, mods, i + 1, 0, do_ln=True, emit_h=True)
    return xs[None]
```

```python
import functools
import math

import numpy as np
import jax
import jax.numpy as jnp
from jax import lax
from jax.experimental import pallas as pl
from jax.experimental.pallas import tpu as pltpu

HEAD_DIM = 128
GQA_GROUP = 4
GRID_W = 64
N_MIXERS = 3
FNET_GROUPS = 4
ROPE_THETA = 10000.0
EPS = 1e-6
ROW_GROUP = 256
FFT_MINOR = 128
BF16_SUBLANES = 16
V7X_VMEM_LIMIT_BYTES = 56 * 1024 * 1024

_F32 = jnp.float32
_BF16 = jnp.bfloat16


def _params(n_grid_axes):
    return pltpu.CompilerParams(dimension_semantics=("arbitrary",) * n_grid_axes,
                                vmem_limit_bytes=V7X_VMEM_LIMIT_BYTES)


def _largest_tile(n, unit, cap):
    best = None
    for t in range(unit, min(n, cap) + 1, unit):
        if n % t == 0:
            best = t
    assert best is not None, (n, unit, cap)
    return best


def _silu(x):
    return x * (1.0 / (1.0 + jnp.exp(-x)))


def _ada_kernel(cv_ref, w_ref, b_ref, o_ref):
    s = _silu(cv_ref[...]).astype(_BF16)
    acc = jnp.dot(s, w_ref[...].astype(_BF16), preferred_element_type=_F32)
    o_ref[...] = acc + b_ref[...]


def _ada_mods(cvec, ada_w, ada_b):
    depth, d, n = ada_w.shape
    tn = _largest_tile(n, 128, 1024)
    return pl.pallas_call(
        _ada_kernel,
        grid=(depth, n // tn),
        in_specs=[pl.BlockSpec((8, d), lambda l, j: (0, 0)),
                  pl.BlockSpec((None, d, tn), lambda l, j: (l, 0, j)),
                  pl.BlockSpec((None, 1, tn), lambda l, j: (l, 0, j))],
        out_specs=pl.BlockSpec((None, 8, tn), lambda l, j: (l, 0, j)),
        out_shape=jax.ShapeDtypeStruct((depth, 8, n), _F32),
        compiler_params=_params(2),
        name="ada_mods",
    )(cvec, ada_w, ada_b.reshape(depth, 1, n))


def _ln_mod_kernel(*refs, do_ln, emit_h, n_sub, n_lat_groups):
    it = iter(refs)
    z_ref = next(it)
    g_ref = next(it) if do_ln else None
    b_ref = next(it) if do_ln else None
    sh_ref = next(it) if emit_h else None
    sc_ref = next(it) if emit_h else None
    x_ref = next(it) if do_ln else None
    h_ref = next(it) if emit_h else None
    i = pl.program_id(0)
    for s in range(n_sub):
        rows = slice(s * ROW_GROUP, (s + 1) * ROW_GROUP)
        x = z_ref[rows, :]
        if do_ln:
            mu = jnp.mean(x, axis=-1, keepdims=True)
            xc = x - mu
            var = jnp.mean(xc * xc, axis=-1, keepdims=True)
            x = xc * lax.rsqrt(var + EPS) * g_ref[...] + b_ref[...]
            x_ref[rows, :] = x
        if emit_h:
            is_ctx = (i * n_sub + s) >= n_lat_groups
            sh = jnp.where(is_ctx, sh_ref[1:2, :], sh_ref[0:1, :])
            sc = jnp.where(is_ctx, sc_ref[1:2, :], sc_ref[0:1, :])
            h_ref[rows, :] = (x * (1.0 + sc) + sh).astype(_BF16)


def _ln_mod(z, n_rows, seq, ln_g2, ln_b2, ln_idx, mods, mod_layer, shift_idx, *, do_ln, emit_h):
    d = z.shape[1]
    tm = _largest_tile(n_rows, ROW_GROUP, 1024)
    in_specs = [pl.BlockSpec((tm, d), lambda i: (i, 0))]
    args = [z]
    if do_ln:
        in_specs += [pl.BlockSpec((None, 1, d), lambda i: (ln_idx, 0, 0))] * 2
        args += [ln_g2, ln_b2]
    if emit_h:
        in_specs += [pl.BlockSpec((None, 8, d), lambda i: (mod_layer, 0, shift_idx)),
                     pl.BlockSpec((None, 8, d), lambda i: (mod_layer, 0, shift_idx + 1))]
        args += [mods, mods]
    out_specs, out_shape = [], []
    if do_ln:
        out_specs.append(pl.BlockSpec((tm, d), lambda i: (i, 0)))
        out_shape.append(jax.ShapeDtypeStruct((n_rows, d), _F32))
    if emit_h:
        out_specs.append(pl.BlockSpec((tm, d), lambda i: (i, 0)))
        out_shape.append(jax.ShapeDtypeStruct((n_rows, d), _BF16))
    outs = pl.pallas_call(
        functools.partial(_ln_mod_kernel, do_ln=do_ln, emit_h=emit_h, n_sub=tm // ROW_GROUP,
                          n_lat_groups=seq // ROW_GROUP),
        grid=(n_rows // tm,),
        in_specs=in_specs, out_specs=out_specs, out_shape=out_shape,
        compiler_params=_params(1),
        name="ln_mod",
    )(*args)
    return outs


def _rope(y, cos, sin_signed):
    lane = lax.broadcasted_iota(jnp.int32, y.shape, 1)
    partner = jnp.where((lane % 64) < 32, pltpu.roll(y, 96, 1), pltpu.roll(y, 32, 1))
    return y * cos + partner * sin_signed


def _head_proj_kernel(*refs, rope, transpose_out, has_norm, out_scale):
    it = iter(refs)
    h_ref, w_ref = next(it), next(it)
    gain_ref = next(it) if has_norm else None
    cos_ref = next(it) if rope else None
    sin_ref = next(it) if rope else None
    o_ref, wbf_ref = next(it), next(it)

    @pl.when(pl.program_id(1) == 0)
    def _():
        wbf_ref[...] = w_ref[...].astype(_BF16)

    acc = jnp.dot(h_ref[...], wbf_ref[...], preferred_element_type=_F32)
    for hh in range(acc.shape[1] // HEAD_DIM):
        cols = slice(hh * HEAD_DIM, (hh + 1) * HEAD_DIM)
        y = acc[:, cols]
        if has_norm:
            ms = jnp.mean(y * y, axis=-1, keepdims=True)
            y = y * lax.rsqrt(ms + EPS) * gain_ref[...]
        if rope:
            y = _rope(y, cos_ref[...], sin_ref[...])
        if out_scale != 1.0:
            y = y * out_scale
        if transpose_out:
            o_ref[cols, :] = y.T.astype(_BF16)
        else:
            o_ref[:, cols] = y.astype(_BF16)


def _head_proj(h, w, w_layer, col0, n_cols, *, transpose_out, rope_tabs=None, gain=None, out_scale=1.0):
    rows, d = h.shape
    tm = _largest_tile(rows, ROW_GROUP, 1024)
    tn = _largest_tile(math.gcd(n_cols, col0) if col0 else n_cols, HEAD_DIM, 1024)
    j0 = col0 // tn
    in_specs = [pl.BlockSpec((tm, d), lambda j, i: (i, 0)),
                pl.BlockSpec((None, d, tn), lambda j, i: (w_layer, 0, j0 + j))]
    args = [h, w]
    if gain is not None:
        in_specs.append(pl.BlockSpec((None, 1, HEAD_DIM), lambda j, i: (w_layer, 0, 0)))
        args.append(gain)
    if rope_tabs is not None:
        in_specs += [pl.BlockSpec((tm, HEAD_DIM), lambda j, i: (i, 0))] * 2
        args += list(rope_tabs)
    if transpose_out:
        out_spec = pl.BlockSpec((tn, tm), lambda j, i: (j, i))
        out_shape = jax.ShapeDtypeStruct((n_cols, rows), _BF16)
    else:
        out_spec = pl.BlockSpec((tm, tn), lambda j, i: (i, j))
        out_shape = jax.ShapeDtypeStruct((rows, n_cols), _BF16)
    return pl.pallas_call(
        functools.partial(_head_proj_kernel, rope=rope_tabs is not None, transpose_out=transpose_out,
                          has_norm=gain is not None, out_scale=out_scale),
        grid=(n_cols // tn, rows // tm),
        in_specs=in_specs, out_specs=out_spec, out_shape=out_shape,
        scratch_shapes=[pltpu.VMEM((d, tn), _BF16)],
        compiler_params=_params(2),
        name="head_proj",
    )(*args)


def _resid_kernel(*refs, alpha, has_bias, n_sub, n_lat_groups):
    it = iter(refs)
    a_ref, w_ref = next(it), next(it)
    b_ref = next(it) if has_bias else None
    x_ref, g_ref, z_ref, wbf_ref = next(it), next(it), next(it), next(it)
    i = pl.program_id(1)

    @pl.when(i == 0)
    def _():
        wbf_ref[...] = w_ref[...].astype(_BF16)

    acc = jnp.dot(a_ref[...], wbf_ref[...], preferred_element_type=_F32)
    if has_bias:
        acc = acc + b_ref[...]
    for s in range(n_sub):
        rows = slice(s * ROW_GROUP, (s + 1) * ROW_GROUP)
        is_ctx = (i * n_sub + s) >= n_lat_groups
        gate = jnp.where(is_ctx, g_ref[1:2, :], g_ref[0:1, :])
        z_ref[rows, :] = alpha * x_ref[rows, :] + gate * acc[rows, :]


def _resid_proj(a, w, w_layer, bias, xs, mods, mod_layer, gate_idx, seq, alpha):
    rows, k = a.shape
    d = w.shape[2]
    tm = _largest_tile(rows, ROW_GROUP, 1024)
    tn = _largest_tile(d, 128, 1024 if k <= 2048 else 512)
    gate_blk0 = gate_idx * (d // tn)
    in_specs = [pl.BlockSpec((tm, k), lambda j, i: (i, 0)),
                pl.BlockSpec((None, k, tn), lambda j, i: (w_layer, 0, j))]
    args = [a, w]
    if bias is not None:
        in_specs.append(pl.BlockSpec((None, 1, tn), lambda j, i: (w_layer, 0, j)))
        args.append(bias)
    in_specs += [pl.BlockSpec((tm, tn), lambda j, i: (i, j)),
                 pl.BlockSpec((None, 8, tn), lambda j, i: (mod_layer, 0, gate_blk0 + j))]
    args += [xs, mods]
    return pl.pallas_call(
        functools.partial(_resid_kernel, alpha=alpha, has_bias=bias is not None, n_sub=tm // ROW_GROUP,
                          n_lat_groups=seq // ROW_GROUP),
        grid=(d // tn, rows // tm),
        in_specs=in_specs,
        out_specs=pl.BlockSpec((tm, tn), lambda j, i: (i, j)),
        out_shape=jax.ShapeDtypeStruct((rows, d), _F32),
        scratch_shapes=[pltpu.VMEM((k, tn), _BF16)],
        compiler_params=_params(2),
        name="resid_proj",
    )(*args)


def _ffn_up_kernel(hp_ref, hm_ref, hn_ref, wv_ref, wg_ref, cwv_ref, cwg_ref, cbv_ref, cbg_ref, o_ref,
                   wv_bf, wg_bf, hs_ref, *, seq, rows):
    i = pl.program_id(1)
    tm = hm_ref.shape[0]
    halo = BF16_SUBLANES

    @pl.when(i == 0)
    def _():
        wv_bf[...] = wv_ref[...].astype(_BF16)
        wg_bf[...] = wg_ref[...].astype(_BF16)

    hs_ref[0:halo, :] = hp_ref[...]
    hs_ref[halo:halo + tm, :] = hm_ref[...]
    hs_ref[halo + tm:, :] = hn_ref[...]
    hs = hs_ref[...]
    r = i * tm + lax.broadcasted_iota(jnp.int32, (tm, 1), 0)
    keep_prev = jnp.where(r == 0, 0.0, 1.0) * jnp.where(r == seq, 0.0, 1.0)
    keep_next = jnp.where(r == seq - 1, 0.0, 1.0) * jnp.where(r == rows - 1, 0.0, 1.0)

    def conv_branch(w_bf, cw_ref, cb_ref):
        u = jnp.dot(hs, w_bf[...], preferred_element_type=_F32)
        cw = cw_ref[...]
        return (cw[1:2, :] * u[halo:halo + tm, :]
                + keep_prev * (cw[0:1, :] * u[halo - 1:halo - 1 + tm, :])
                + keep_next * (cw[2:3, :] * u[halo + 1:halo + 1 + tm, :])
                + cb_ref[...])

    val = conv_branch(wv_bf, cwv_ref, cbv_ref)
    gate = conv_branch(wg_bf, cwg_ref, cbg_ref)
    o_ref[...] = (_silu(gate) * val).astype(_BF16)


def _ffn_up(h, w_up, conv_w, conv_b3, layer, seq):
    rows, d = h.shape
    d_ff = w_up.shape[2] // 2
    tm = _largest_tile(rows, ROW_GROUP, 1024)
    tf = _largest_tile(d_ff, 128, 512)
    nf = d_ff // tf
    halo = BF16_SUBLANES
    per_tile = tm // halo
    last_halo_blk = rows // halo - 1
    return pl.pallas_call(
        functools.partial(_ffn_up_kernel, seq=seq, rows=rows),
        grid=(nf, rows // tm),
        in_specs=[pl.BlockSpec((halo, d), lambda j, i: (jnp.maximum(i * per_tile - 1, 0), 0)),
                  pl.BlockSpec((tm, d), lambda j, i: (i, 0)),
                  pl.BlockSpec((halo, d), lambda j, i: (jnp.minimum((i + 1) * per_tile, last_halo_blk), 0)),
                  pl.BlockSpec((None, d, tf), lambda j, i: (layer, 0, j)),
                  pl.BlockSpec((None, d, tf), lambda j, i: (layer, 0, nf + j)),
                  pl.BlockSpec((None, 3, tf), lambda j, i: (layer, 0, j)),
                  pl.BlockSpec((None, 3, tf), lambda j, i: (layer, 0, nf + j)),
                  pl.BlockSpec((None, 1, tf), lambda j, i: (layer, 0, j)),
                  pl.BlockSpec((None, 1, tf), lambda j, i: (layer, 0, nf + j))],
        out_specs=pl.BlockSpec((tm, tf), lambda j, i: (i, j)),
        out_shape=jax.ShapeDtypeStruct((rows, d_ff), _BF16),
        scratch_shapes=[pltpu.VMEM((d, tf), _BF16), pltpu.VMEM((d, tf), _BF16),
                        pltpu.VMEM((tm + 2 * halo, d), _BF16)],
        compiler_params=_params(2),
        name="ffn_up",
    )(h, h, h, w_up, w_up, conv_w, conv_w, conv_b3, conv_b3)


_NEG_BIG = -1e30
_LOG2E = math.log2(math.e)


def _attend_t(qt_list, k_slices, vt_slice, acc_refs, s_ref, *, n_main, seq, ctx, tk):
    n_maps = len(qt_list)
    tq = qt_list[0].shape[1]

    def scores(rows, c):
        return jnp.dot(k_slices[c](rows), qt_list[c], preferred_element_type=_F32)

    def absorb(s, vt, c, m, l, first):
        m_cur = jnp.max(s, axis=0, keepdims=True)
        m_new = m_cur if first else jnp.maximum(m, m_cur)
        p = jnp.exp2(s - m_new)
        pv = jnp.dot(vt, p.astype(_BF16), preferred_element_type=_F32)
        if first:
            acc_refs[c][...] = pv
            return m_new, jnp.sum(p, axis=0, keepdims=True)
        alpha = jnp.exp2(m - m_new)
        acc_refs[c][...] = alpha * acc_refs[c][...] + pv
        return m_new, alpha * l + jnp.sum(p, axis=0, keepdims=True)

    def latent_rows(c):
        return pl.ds(pl.multiple_of(c * tk, tk), tk)

    ctx_rows = slice(seq, seq + ctx)
    vt_ctx = vt_slice(ctx_rows)
    carry = tuple(absorb(scores(ctx_rows, c), vt_ctx, c, None, None, True) for c in range(n_maps))
    for c in range(n_maps):
        s_ref[0, c] = scores(latent_rows(0), c)

    def body(j, carry):
        for half in range(2):
            cur = 2 * j + half
            nxt = latent_rows(jnp.minimum(cur + 1, n_main - 1))
            for c in range(n_maps):
                s_ref[1 - half, c] = scores(nxt, c)
            vt = vt_slice(latent_rows(cur))
            carry = tuple(absorb(s_ref[half, c], vt, c, *carry[c], False) for c in range(n_maps))
        return carry

    carry = lax.fori_loop(0, n_main // 2, body, carry)
    return [l for (_, l) in carry]


def _gqa_attn_kernel(qt_ref, k_ref, vt_ref, o_ref, s_ref, *acc_refs, seq, ctx, tk):
    i = pl.program_id(1)
    tq = qt_ref.shape[1]
    n_main = jnp.where(i < seq // tq, seq // tk, 0)
    qt_list = [qt_ref[g * HEAD_DIM:(g + 1) * HEAD_DIM, :] for g in range(GQA_GROUP)]
    k_slices = [lambda rows: k_ref[rows, :]] * GQA_GROUP
    sums = _attend_t(qt_list, k_slices, lambda rows: vt_ref[:, rows], acc_refs, s_ref,
                     n_main=n_main, seq=seq, ctx=ctx, tk=tk)
    for g in range(GQA_GROUP):
        o_t = acc_refs[g][...] * (1.0 / sums[g])
        o_ref[:, g * HEAD_DIM:(g + 1) * HEAD_DIM] = o_t.T.astype(_BF16)


def _gqa_attention(q_t, k, v_t, seq, ctx):
    dq, rows = q_t.shape
    kvh = k.shape[1] // HEAD_DIM
    tq = ROW_GROUP
    qw = GQA_GROUP * HEAD_DIM
    tk = _largest_tile(seq, 128, 512)
    return pl.pallas_call(
        functools.partial(_gqa_attn_kernel, seq=seq, ctx=ctx, tk=tk),
        grid=(kvh, rows // tq),
        in_specs=[pl.BlockSpec((qw, tq), lambda g, i: (g, i)),
                  pl.BlockSpec((rows, HEAD_DIM), lambda g, i: (0, g)),
                  pl.BlockSpec((HEAD_DIM, rows), lambda g, i: (g, 0))],
        out_specs=pl.BlockSpec((tq, qw), lambda g, i: (i, g)),
        out_shape=jax.ShapeDtypeStruct((rows, dq), _BF16),
        scratch_shapes=[pltpu.VMEM((2, GQA_GROUP, tk, tq), _F32)] + [pltpu.VMEM((HEAD_DIM, tq), _F32)] * GQA_GROUP,
        compiler_params=_params(2),
        name="gqa_attention",
    )(q_t, k, v_t)


def _diff_attn_kernel(qt_ref, k_ref, vt_ref, lam_ref, sub_ref, o_ref, s_ref, *acc_refs, seq, ctx, tk, lambda_init):
    i = pl.program_id(1)
    tq = qt_ref.shape[1]
    n_main = jnp.where(i < seq // tq, seq // tk, 0)
    qt_list = [qt_ref[c * HEAD_DIM:(c + 1) * HEAD_DIM, :] for c in range(2)]
    k_slices = [functools.partial(lambda rows, c: k_ref[rows, c * HEAD_DIM:(c + 1) * HEAD_DIM], c=c)
                for c in range(2)]
    l0, l1 = _attend_t(qt_list, k_slices, lambda rows: vt_ref[:, rows], acc_refs, s_ref,
                       n_main=n_main, seq=seq, ctx=ctx, tk=tk)
    lp = lam_ref[...]
    lam = (jnp.exp(jnp.sum(lp[0:1, :] * lp[1:2, :], axis=1, keepdims=True))
           - jnp.exp(jnp.sum(lp[2:3, :] * lp[3:4, :], axis=1, keepdims=True)) + lambda_init)
    o_t = acc_refs[0][...] * (1.0 / l0) - lam * (acc_refs[1][...] * (1.0 / l1))
    o = o_t.T
    ms = jnp.mean(o * o, axis=-1, keepdims=True)
    o = o * lax.rsqrt(ms + EPS) * sub_ref[...] * (1.0 - lambda_init)
    o_ref[...] = o.astype(_BF16)


def _diff_attention(q_t, k, v_t, lam_params, subln, layer_j, seq, ctx, lambda_init):
    d, rows = q_t.shape
    hw = 2 * HEAD_DIM
    tq = ROW_GROUP
    tk = _largest_tile(seq, 128, 512)
    return pl.pallas_call(
        functools.partial(_diff_attn_kernel, seq=seq, ctx=ctx, tk=tk, lambda_init=lambda_init),
        grid=(d // hw, rows // tq),
        in_specs=[pl.BlockSpec((hw, tq), lambda h, i: (h, i)),
                  pl.BlockSpec((rows, hw), lambda h, i: (0, h)),
                  pl.BlockSpec((hw, rows), lambda h, i: (h, 0)),
                  pl.BlockSpec((None, 4, HEAD_DIM), lambda h, i: (layer_j, 0, 0)),
                  pl.BlockSpec((None, 1, hw), lambda h, i: (layer_j, 0, 0))],
        out_specs=pl.BlockSpec((tq, hw), lambda h, i: (i, h)),
        out_shape=jax.ShapeDtypeStruct((rows, d), _BF16),
        scratch_shapes=[pltpu.VMEM((2, 2, tk, tq), _F32)] + [pltpu.VMEM((hw, tq), _F32)] * 2,
        compiler_params=_params(2),
        name="diff_attention",
    )(q_t, k, v_t, lam_params, subln)


def _dft_cos_sin(n):
    kt = np.outer(np.arange(n), np.arange(n)) % n
    ang = 2.0 * np.pi * kt / n
    return np.cos(ang), np.sin(ang)


def _channel_dft(z, cs_ref, scale, store):
    p = z.shape[0] // 2
    gd = cs_ref.shape[1]
    zb = z.astype(_BF16)
    for g in range(z.shape[1] // gd):
        cols = slice(g * gd, (g + 1) * gd)
        zc = jnp.concatenate([zb[:p, cols], zb[p:, cols]], axis=1)
        y = jnp.dot(zc, cs_ref[...], preferred_element_type=_F32) * scale
        store(cols, y.astype(_BF16))


def _fft_stage1_kernel(a_ref, f1_ref, tw_ref, o_ref, *, d):
    n1 = a_ref.shape[0]
    for tt in range(a_ref.shape[1] // d):
        cols = slice(tt * d, (tt + 1) * d)
        a = jnp.dot(f1_ref[...], a_ref[:, cols], preferred_element_type=_F32)
        ar, ai = a[:n1, :], a[n1:, :]
        twr, twi = tw_ref[0, :, tt:tt + 1], tw_ref[1, :, tt:tt + 1]
        o_ref[0, :, cols] = (ar * twr - ai * twi).astype(_BF16)
        o_ref[1, :, cols] = (ar * twi + ai * twr).astype(_BF16)


def _fft_stage2_kernel(b_ref, g2_ref, cs_ref, y_ref, *, scale):
    d = b_ref.shape[3]
    for kk in range(b_ref.shape[1]):
        bcat = jnp.concatenate([b_ref[0, kk], b_ref[1, kk]], axis=0)
        z = jnp.dot(g2_ref[...], bcat, preferred_element_type=_F32)

        def store(cols, y, kk=kk):
            y_ref[:, kk * d + cols.start:kk * d + cols.stop] = y

        _channel_dft(z, cs_ref, scale, store)


def _ctx_dft_kernel(a_ref, fc_ref, cs_ref, y_ref, *, scale):
    z = jnp.dot(fc_ref[...], a_ref[...], preferred_element_type=_F32)

    def store(cols, y):
        y_ref[:, cols] = y

    _channel_dft(z, cs_ref, scale, store)


def _fourier_mix(h, seq, ctx):
    rows, d = h.shape
    gd = d // FNET_GROUPS
    n2 = FFT_MINOR
    n1 = seq // n2
    g2 = 16
    kb = 2
    assert seq % n2 == 0 and rows % n2 == 0 and n2 % g2 == 0 and n1 % kb == 0

    c1, s1 = _dft_cos_sin(n1)
    f1 = jnp.asarray(np.concatenate([c1, -s1], axis=0), _BF16)
    ang = 2.0 * np.pi * np.outer(np.arange(n1), np.arange(n2)) / seq
    tw = np.stack([np.cos(ang), -np.sin(ang)])
    tw = jnp.asarray(tw.reshape(2, n1, n2 // g2, g2).transpose(2, 0, 1, 3), _F32)
    c2, s2 = _dft_cos_sin(n2)
    g2m = jnp.asarray(np.block([[c2, s2], [-s2, c2]]), _BF16)
    cc, sc = _dft_cos_sin(gd)
    cs = jnp.asarray(np.concatenate([cc, sc], axis=0), _BF16)
    cx, sx = _dft_cos_sin(ctx)
    fc = jnp.asarray(np.concatenate([cx, -sx], axis=0), _BF16)

    b = pl.pallas_call(
        functools.partial(_fft_stage1_kernel, d=d),
        grid=(n2 // g2,),
        in_specs=[pl.BlockSpec((n1, g2 * d), lambda c: (0, c)),
                  pl.BlockSpec((2 * n1, n1), lambda c: (0, 0)),
                  pl.BlockSpec((None, 2, n1, g2), lambda c: (c, 0, 0, 0))],
        out_specs=pl.BlockSpec((2, n1, g2 * d), lambda c: (0, 0, c)),
        out_shape=jax.ShapeDtypeStruct((2, n1, n2 * d), _BF16),
        compiler_params=_params(1),
        name="fft_stage1",
    )(h.reshape(rows // n2, n2 * d), f1, tw)

    y_lat = pl.pallas_call(
        functools.partial(_fft_stage2_kernel, scale=1.0 / math.sqrt(seq * gd)),
        grid=(n1 // kb,),
        in_specs=[pl.BlockSpec((2, kb, n2, d), lambda k: (0, k, 0, 0)),
                  pl.BlockSpec((2 * n2, 2 * n2), lambda k: (0, 0)),
                  pl.BlockSpec((2 * gd, gd), lambda k: (0, 0))],
        out_specs=pl.BlockSpec((n2, kb * d), lambda k: (0, k)),
        out_shape=jax.ShapeDtypeStruct((n2, n1 * d), _BF16),
        compiler_params=_params(1),
        name="fft_stage2",
    )(b.reshape(2, n1, n2, d), g2m, cs)

    y_ctx = pl.pallas_call(
        functools.partial(_ctx_dft_kernel, scale=1.0 / math.sqrt(ctx * gd)),
        grid=(1,),
        in_specs=[pl.BlockSpec((ctx, d), lambda c: (seq // ctx, 0)),
                  pl.BlockSpec((2 * ctx, ctx), lambda c: (0, 0)),
                  pl.BlockSpec((2 * gd, gd), lambda c: (0, 0))],
        out_specs=pl.BlockSpec((ctx, d), lambda c: (0, 0)),
        out_shape=jax.ShapeDtypeStruct((ctx, d), _BF16),
        compiler_params=_params(1),
        name="ctx_dft",
    )(h, fc, cs)
    return jnp.concatenate([y_lat.reshape(seq, d), y_ctx], axis=0)


def _rope_tables(seq, ctx):
    quarter = HEAD_DIM // 4
    t = np.arange(seq)
    inv_freq = ROPE_THETA ** (-np.arange(quarter, dtype=np.float64) / quarter)
    ang_r = (t // GRID_W)[:, None] * inv_freq[None, :]
    ang_c = (t % GRID_W)[:, None] * inv_freq[None, :]
    cos = np.concatenate([np.cos(ang_r)] * 2 + [np.cos(ang_c)] * 2, axis=1)
    sin = np.concatenate([-np.sin(ang_r), np.sin(ang_r), -np.sin(ang_c), np.sin(ang_c)], axis=1)
    cos = np.concatenate([cos, np.ones((ctx, HEAD_DIM))], axis=0)
    sin = np.concatenate([sin, np.zeros((ctx, HEAD_DIM))], axis=0)
    return jnp.asarray(cos, _F32), jnp.asarray(sin, _F32)


def kernel(x, c, ctx, c_ctx, ada_w, ada_b, ln_g, ln_b, gqa_w_qkv, gqa_q_norm, gqa_k_norm, gqa_w_o,
           diff_w_qkv, diff_lambda, diff_subln, diff_w_o, fnet_w, fnet_b,
           ffn_w_up, ffn_conv_w, ffn_conv_b, ffn_w_down):
    batch, seq, d = x.shape
    n_ctx = ctx.shape[1]
    depth = ada_w.shape[0]
    assert batch == 1 and seq % ROW_GROUP == 0 and n_ctx % ROW_GROUP == 0 and seq % GRID_W == 0
    alpha = (2 * depth) ** 0.25
    q_scale = HEAD_DIM ** -0.5 * _LOG2E

    cvec = jnp.concatenate([c, c_ctx[None, :], jnp.zeros((6, d), _F32)], axis=0)
    mods = _ada_mods(cvec, ada_w, ada_b)
    xs = jnp.concatenate([x[0], ctx[0]], axis=0)
    rows = seq + n_ctx
    rope_tabs = _rope_tables(seq, n_ctx)
    ln_g2 = ln_g.reshape(depth * 2, 1, d)
    ln_b2 = ln_b.reshape(depth * 2, 1, d)
    conv_b3 = ffn_conv_b.reshape(depth, 1, -1)
    gqa_q_norm3 = gqa_q_norm.reshape(-1, 1, HEAD_DIM)
    gqa_k_norm3 = gqa_k_norm.reshape(-1, 1, HEAD_DIM)
    diff_subln3 = diff_subln.reshape(-1, 1, 2 * HEAD_DIM)
    fnet_b3 = fnet_b.reshape(-1, 1, d)

    (h,) = _ln_mod(xs, rows, seq, None, None, 0, mods, 0, 0, do_ln=False, emit_h=True)
    counts = [0, 0, 0]
    for i in range(depth):
        last = i == depth - 1
        kind = i % N_MIXERS
        j = counts[kind]
        counts[kind] += 1
        if kind == 0:
            n_q = d
            n_kv = d // GQA_GROUP
            q_t = _head_proj(h, gqa_w_qkv, j, 0, n_q, transpose_out=True, rope_tabs=rope_tabs,
                             gain=gqa_q_norm3, out_scale=q_scale)
            k = _head_proj(h, gqa_w_qkv, j, n_q, n_kv, transpose_out=False, rope_tabs=rope_tabs, gain=gqa_k_norm3)
            v_t = _head_proj(h, gqa_w_qkv, j, n_q + n_kv, n_kv, transpose_out=True)
            o = _gqa_attention(q_t, k, v_t, seq, n_ctx)
            z = _resid_proj(o, gqa_w_o, j, None, xs, mods, i, 2, seq, alpha)
        elif kind == 1:
            lambda_init = 0.8 - 0.6 * math.exp(-0.3 * i)
            q_t = _head_proj(h, diff_w_qkv, j, 0, d, transpose_out=True, rope_tabs=rope_tabs, out_scale=q_scale)
            k = _head_proj(h, diff_w_qkv, j, d, d, transpose_out=False, rope_tabs=rope_tabs)
            v_t = _head_proj(h, diff_w_qkv, j, 2 * d, d, transpose_out=True)
            o = _diff_attention(q_t, k, v_t, diff_lambda, diff_subln3, j, seq, n_ctx, lambda_init)
            z = _resid_proj(o, diff_w_o, j, None, xs, mods, i, 2, seq, alpha)
        else:
            y = _fourier_mix(h, seq, n_ctx)
            z = _resid_proj(y, fnet_w, j, fnet_b3, xs, mods, i, 2, seq, alpha)
        xs, h = _ln_mod(z, rows, seq, ln_g2, ln_b2, 2 * i, mods, i, 3, do_ln=True, emit_h=True)
        a = _ffn_up(h, ffn_w_up, ffn_conv_w, conv_b3, i, seq)
        z = _resid_proj(a, ffn_w_down, i, None, xs, mods, i, 5, seq, alpha)
        if last:
            (xs,) = _ln_mod(z, seq, seq, ln_g2, ln_b2, 2 * i + 1, mods, 0, 0, do_ln=True, emit_h=False)
        else:
            xs, h = _ln_mod(z, rows, seq, ln_g2, ln_b2, 2 * i + 1, mods, i + 1, 0, do_ln=True, emit_h=True)
    return xs[None]
```

```python
import functools
import math

import numpy as np
import jax
import jax.numpy as jnp
from jax import lax
from jax.experimental import pallas as pl
from jax.experimental.pallas import tpu as pltpu

HEAD_DIM = 128
GQA_GROUP = 4
GRID_W = 64
N_MIXERS = 3
FNET_GROUPS = 4
ROPE_THETA = 10000.0
EPS = 1e-6
ROW_GROUP = 256
FFT_MINOR = 128
BF16_SUBLANES = 16
V7X_VMEM_LIMIT_BYTES = 56 * 1024 * 1024

_F32 = jnp.float32
_BF16 = jnp.bfloat16


def _params(n_grid_axes):
    return pltpu.CompilerParams(dimension_semantics=("arbitrary",) * n_grid_axes,
                                vmem_limit_bytes=V7X_VMEM_LIMIT_BYTES)


def _largest_tile(n, unit, cap):
    best = None
    for t in range(unit, min(n, cap) + 1, unit):
        if n % t == 0:
            best = t
    assert best is not None, (n, unit, cap)
    return best


def _silu(x):
    return x * (1.0 / (1.0 + jnp.exp(-x)))


def _ada_kernel(cv_ref, w_ref, b_ref, o_ref):
    s = _silu(cv_ref[...]).astype(_BF16)
    acc = jnp.dot(s, w_ref[...].astype(_BF16), preferred_element_type=_F32)
    o_ref[...] = acc + b_ref[...]


def _ada_mods(cvec, ada_w, ada_b):
    depth, d, n = ada_w.shape
    tn = _largest_tile(n, 128, 1024)
    return pl.pallas_call(
        _ada_kernel,
        grid=(depth, n // tn),
        in_specs=[pl.BlockSpec((8, d), lambda l, j: (0, 0)),
                  pl.BlockSpec((None, d, tn), lambda l, j: (l, 0, j)),
                  pl.BlockSpec((None, 1, tn), lambda l, j: (l, 0, j))],
        out_specs=pl.BlockSpec((None, 8, tn), lambda l, j: (l, 0, j)),
        out_shape=jax.ShapeDtypeStruct((depth, 8, n), _F32),
        compiler_params=_params(2),
        name="ada_mods",
    )(cvec, ada_w, ada_b.reshape(depth, 1, n))


def _ln_mod_kernel(*refs, do_ln, emit_h, n_sub, n_lat_groups):
    it = iter(refs)
    z_ref = next(it)
    g_ref = next(it) if do_ln else None
    b_ref = next(it) if do_ln else None
    sh_ref = next(it) if emit_h else None
    sc_ref = next(it) if emit_h else None
    x_ref = next(it) if do_ln else None
    h_ref = next(it) if emit_h else None
    i = pl.program_id(0)
    for s in range(n_sub):
        rows = slice(s * ROW_GROUP, (s + 1) * ROW_GROUP)
        x = z_ref[rows, :]
        if do_ln:
            mu = jnp.mean(x, axis=-1, keepdims=True)
            xc = x - mu
            var = jnp.mean(xc * xc, axis=-1, keepdims=True)
            x = xc * lax.rsqrt(var + EPS) * g_ref[...] + b_ref[...]
            x_ref[rows, :] = x
        if emit_h:
            is_ctx = (i * n_sub + s) >= n_lat_groups
            sh = jnp.where(is_ctx, sh_ref[1:2, :], sh_ref[0:1, :])
            sc = jnp.where(is_ctx, sc_ref[1:2, :], sc_ref[0:1, :])
            h_ref[rows, :] = (x * (1.0 + sc) + sh).astype(_BF16)


def _ln_mod(z, n_rows, seq, ln_g2, ln_b2, ln_idx, mods, mod_layer, shift_idx, *, do_ln, emit_h):
    d = z.shape[1]
    tm = _largest_tile(n_rows, ROW_GROUP, 1024)
    in_specs = [pl.BlockSpec((tm, d), lambda i: (i, 0))]
    args = [z]
    if do_ln:
        in_specs += [pl.BlockSpec((None, 1, d), lambda i: (ln_idx, 0, 0))] * 2
        args += [ln_g2, ln_b2]
    if emit_h:
        in_specs += [pl.BlockSpec((None, 8, d), lambda i: (mod_layer, 0, shift_idx)),
                     pl.BlockSpec((None, 8, d), lambda i: (mod_layer, 0, shift_idx + 1))]
        args += [mods, mods]
    out_specs, out_shape = [], []
    if do_ln:
        out_specs.append(pl.BlockSpec((tm, d), lambda i: (i, 0)))
        out_shape.append(jax.ShapeDtypeStruct((n_rows, d), _F32))
    if emit_h:
        out_specs.append(pl.BlockSpec((tm, d), lambda i: (i, 0)))
        out_shape.append(jax.ShapeDtypeStruct((n_rows, d), _BF16))
    outs = pl.pallas_call(
        functools.partial(_ln_mod_kernel, do_ln=do_ln, emit_h=emit_h, n_sub=tm // ROW_GROUP,
                          n_lat_groups=seq // ROW_GROUP),
        grid=(n_rows // tm,),
        in_specs=in_specs, out_specs=out_specs, out_shape=out_shape,
        compiler_params=_params(1),
        name="ln_mod",
    )(*args)
    return outs


def _rope(y, cos, sin_signed):
    lane = lax.broadcasted_iota(jnp.int32, y.shape, 1)
    partner = jnp.where((lane % 64) < 32, pltpu.roll(y, 96, 1), pltpu.roll(y, 32, 1))
    return y * cos + partner * sin_signed


def _lagged_steps(i, n_tiles, produce, consume):
    @pl.when(jnp.logical_and(i < n_tiles, i % 2 == 0))
    def _():
        consume(1)
        produce(0)

    @pl.when(jnp.logical_and(i < n_tiles, i % 2 == 1))
    def _():
        consume(0)
        produce(1)

    @pl.when(i == n_tiles)
    def _():
        consume((n_tiles - 1) % 2)


def _lagged_index_maps(n_tiles):
    return (lambda i: jnp.minimum(i, n_tiles - 1)), (lambda i: jnp.maximum(i - 1, 0))


def _head_proj_kernel(*refs, n_tiles, rope, transpose_out, has_norm, out_scale):
    it = iter(refs)
    h_ref, w_ref = next(it), next(it)
    gain_ref = next(it) if has_norm else None
    cos_ref = next(it) if rope else None
    sin_ref = next(it) if rope else None
    o_ref, wbf_ref, raw_ref = next(it), next(it), next(it)
    i = pl.program_id(1)

    @pl.when(i == 0)
    def _():
        wbf_ref[...] = w_ref[...].astype(_BF16)
        raw_ref[1] = jnp.zeros(raw_ref.shape[1:], _F32)

    def produce(slot):
        raw_ref[slot] = jnp.dot(h_ref[...], wbf_ref[...], preferred_element_type=_F32)

    def consume(slot):
        for hh in range(raw_ref.shape[2] // HEAD_DIM):
            cols = slice(hh * HEAD_DIM, (hh + 1) * HEAD_DIM)
            y = raw_ref[slot, :, cols]
            if has_norm:
                ms = jnp.mean(y * y, axis=-1, keepdims=True)
                y = y * lax.rsqrt(ms + EPS) * gain_ref[...]
            if rope:
                y = _rope(y, cos_ref[...], sin_ref[...])
            if out_scale != 1.0:
                y = y * out_scale
            if transpose_out:
                o_ref[cols, :] = y.T.astype(_BF16)
            else:
                o_ref[:, cols] = y.astype(_BF16)

    _lagged_steps(i, n_tiles, produce, consume)


def _head_proj(h, w, w_layer, col0, n_cols, *, transpose_out, rope_tabs=None, gain=None, out_scale=1.0):
    rows, d = h.shape
    tm = _largest_tile(rows, ROW_GROUP, 1024)
    tn = _largest_tile(math.gcd(n_cols, col0) if col0 else n_cols, HEAD_DIM, 1024)
    j0 = col0 // tn
    n_tiles = rows // tm
    produced, consumed = _lagged_index_maps(n_tiles)
    in_specs = [pl.BlockSpec((tm, d), lambda j, i: (produced(i), 0)),
                pl.BlockSpec((None, d, tn), lambda j, i: (w_layer, 0, j0 + j))]
    args = [h, w]
    if gain is not None:
        in_specs.append(pl.BlockSpec((None, 1, HEAD_DIM), lambda j, i: (w_layer, 0, 0)))
        args.append(gain)
    if rope_tabs is not None:
        in_specs += [pl.BlockSpec((tm, HEAD_DIM), lambda j, i: (consumed(i), 0))] * 2
        args += list(rope_tabs)
    if transpose_out:
        out_spec = pl.BlockSpec((tn, tm), lambda j, i: (j, consumed(i)))
        out_shape = jax.ShapeDtypeStruct((n_cols, rows), _BF16)
    else:
        out_spec = pl.BlockSpec((tm, tn), lambda j, i: (consumed(i), j))
        out_shape = jax.ShapeDtypeStruct((rows, n_cols), _BF16)
    return pl.pallas_call(
        functools.partial(_head_proj_kernel, n_tiles=n_tiles, rope=rope_tabs is not None,
                          transpose_out=transpose_out, has_norm=gain is not None, out_scale=out_scale),
        grid=(n_cols // tn, n_tiles + 1),
        in_specs=in_specs, out_specs=out_spec, out_shape=out_shape,
        scratch_shapes=[pltpu.VMEM((d, tn), _BF16), pltpu.VMEM((2, tm, tn), _F32)],
        compiler_params=_params(2),
        name="head_proj",
    )(*args)


def _resid_kernel(*refs, alpha, has_bias, n_sub, n_lat_groups):
    it = iter(refs)
    a_ref, w_ref = next(it), next(it)
    b_ref = next(it) if has_bias else None
    x_ref, g_ref, z_ref, wbf_ref = next(it), next(it), next(it), next(it)
    i = pl.program_id(1)

    @pl.when(i == 0)
    def _():
        wbf_ref[...] = w_ref[...].astype(_BF16)

    acc = jnp.dot(a_ref[...], wbf_ref[...], preferred_element_type=_F32)
    if has_bias:
        acc = acc + b_ref[...]
    for s in range(n_sub):
        rows = slice(s * ROW_GROUP, (s + 1) * ROW_GROUP)
        is_ctx = (i * n_sub + s) >= n_lat_groups
        gate = jnp.where(is_ctx, g_ref[1:2, :], g_ref[0:1, :])
        z_ref[rows, :] = alpha * x_ref[rows, :] + gate * acc[rows, :]


def _resid_proj(a, w, w_layer, bias, xs, mods, mod_layer, gate_idx, seq, alpha):
    rows, k = a.shape
    d = w.shape[2]
    tm = _largest_tile(rows, ROW_GROUP, 1024)
    tn = _largest_tile(d, 128, 1024 if k <= 2048 else 512)
    gate_blk0 = gate_idx * (d // tn)
    in_specs = [pl.BlockSpec((tm, k), lambda j, i: (i, 0)),
                pl.BlockSpec((None, k, tn), lambda j, i: (w_layer, 0, j))]
    args = [a, w]
    if bias is not None:
        in_specs.append(pl.BlockSpec((None, 1, tn), lambda j, i: (w_layer, 0, j)))
        args.append(bias)
    in_specs += [pl.BlockSpec((tm, tn), lambda j, i: (i, j)),
                 pl.BlockSpec((None, 8, tn), lambda j, i: (mod_layer, 0, gate_blk0 + j))]
    args += [xs, mods]
    return pl.pallas_call(
        functools.partial(_resid_kernel, alpha=alpha, has_bias=bias is not None, n_sub=tm // ROW_GROUP,
                          n_lat_groups=seq // ROW_GROUP),
        grid=(d // tn, rows // tm),
        in_specs=in_specs,
        out_specs=pl.BlockSpec((tm, tn), lambda j, i: (i, j)),
        out_shape=jax.ShapeDtypeStruct((rows, d), _F32),
        scratch_shapes=[pltpu.VMEM((k, tn), _BF16)],
        compiler_params=_params(2),
        name="resid_proj",
    )(*args)


def _ffn_up_kernel(hp_ref, hm_ref, hn_ref, wv_ref, wg_ref, cwv_ref, cwg_ref, cbv_ref, cbg_ref, o_ref,
                   wv_bf, wg_bf, hs_ref, *, seq, rows):
    i = pl.program_id(1)
    tm = hm_ref.shape[0]
    halo = BF16_SUBLANES

    @pl.when(i == 0)
    def _():
        wv_bf[...] = wv_ref[...].astype(_BF16)
        wg_bf[...] = wg_ref[...].astype(_BF16)

    hs_ref[0:halo, :] = hp_ref[...]
    hs_ref[halo:halo + tm, :] = hm_ref[...]
    hs_ref[halo + tm:, :] = hn_ref[...]
    hs = hs_ref[...]
    r = i * tm + lax.broadcasted_iota(jnp.int32, (tm, 1), 0)
    keep_prev = jnp.where(r == 0, 0.0, 1.0) * jnp.where(r == seq, 0.0, 1.0)
    keep_next = jnp.where(r == seq - 1, 0.0, 1.0) * jnp.where(r == rows - 1, 0.0, 1.0)

    def conv_branch(w_bf, cw_ref, cb_ref):
        u = jnp.dot(hs, w_bf[...], preferred_element_type=_F32)
        cw = cw_ref[...]
        return (cw[1:2, :] * u[halo:halo + tm, :]
                + keep_prev * (cw[0:1, :] * u[halo - 1:halo - 1 + tm, :])
                + keep_next * (cw[2:3, :] * u[halo + 1:halo + 1 + tm, :])
                + cb_ref[...])

    val = conv_branch(wv_bf, cwv_ref, cbv_ref)
    gate = conv_branch(wg_bf, cwg_ref, cbg_ref)
    o_ref[...] = (_silu(gate) * val).astype(_BF16)


def _ffn_up(h, w_up, conv_w, conv_b3, layer, seq):
    rows, d = h.shape
    d_ff = w_up.shape[2] // 2
    tm = _largest_tile(rows, ROW_GROUP, 1024)
    tf = _largest_tile(d_ff, 128, 512)
    nf = d_ff // tf
    halo = BF16_SUBLANES
    per_tile = tm // halo
    last_halo_blk = rows // halo - 1
    return pl.pallas_call(
        functools.partial(_ffn_up_kernel, seq=seq, rows=rows),
        grid=(nf, rows // tm),
        in_specs=[pl.BlockSpec((halo, d), lambda j, i: (jnp.maximum(i * per_tile - 1, 0), 0)),
                  pl.BlockSpec((tm, d), lambda j, i: (i, 0)),
                  pl.BlockSpec((halo, d), lambda j, i: (jnp.minimum((i + 1) * per_tile, last_halo_blk), 0)),
                  pl.BlockSpec((None, d, tf), lambda j, i: (layer, 0, j)),
                  pl.BlockSpec((None, d, tf), lambda j, i: (layer, 0, nf + j)),
                  pl.BlockSpec((None, 3, tf), lambda j, i: (layer, 0, j)),
                  pl.BlockSpec((None, 3, tf), lambda j, i: (layer, 0, nf + j)),
                  pl.BlockSpec((None, 1, tf), lambda j, i: (layer, 0, j)),
                  pl.BlockSpec((None, 1, tf), lambda j, i: (layer, 0, nf + j))],
        out_specs=pl.BlockSpec((tm, tf), lambda j, i: (i, j)),
        out_shape=jax.ShapeDtypeStruct((rows, d_ff), _BF16),
        scratch_shapes=[pltpu.VMEM((d, tf), _BF16), pltpu.VMEM((d, tf), _BF16),
                        pltpu.VMEM((tm + 2 * halo, d), _BF16)],
        compiler_params=_params(2),
        name="ffn_up",
    )(h, h, h, w_up, w_up, conv_w, conv_w, conv_b3, conv_b3)


_NEG_BIG = -1e30
_LOG2E = math.log2(math.e)
ATTN_KEY_CHUNK_CAP = 1024


def _key_chunk(seq, ctx):
    tk = max(t for t in range(128, min(seq // 2, ATTN_KEY_CHUNK_CAP) + 1, 128)
             if seq % t == 0 and (seq // t) % 2 == 0)
    assert tk >= ctx and ctx % 128 == 0, (tk, ctx)
    return tk


def _attend_t(qt_list, k_slices, vt_slice, acc_refs, s_ref, *, n_main, seq, ctx, tk):
    n_maps = len(qt_list)
    tq = qt_list[0].shape[1]

    def scores(rows, c):
        return jnp.dot(k_slices[c](rows), qt_list[c], preferred_element_type=_F32)

    def absorb(s, vt, c, m, l):
        m_new = jnp.maximum(m, jnp.max(s, axis=0, keepdims=True))
        p = jnp.exp2(s - m_new)
        alpha = jnp.exp2(m - m_new)
        acc_refs[c][...] = alpha * acc_refs[c][...] + jnp.dot(vt, p.astype(_BF16), preferred_element_type=_F32)
        return m_new, alpha * l + jnp.sum(p, axis=0, keepdims=True)

    tail_start = seq + ctx - tk

    def chunk_rows(n):
        start = jnp.where(n < n_main, n * tk, tail_start)
        return pl.ds(pl.multiple_of(start, math.gcd(tk, tail_start)), tk)

    for acc_ref in acc_refs:
        acc_ref[...] = jnp.zeros(acc_ref.shape, _F32)
    for c in range(n_maps):
        s_ref[0, c] = scores(chunk_rows(0), c)

    def body(j, carry):
        for half in range(2):
            cur = 2 * j + half
            nxt = chunk_rows(cur + 1)
            for c in range(n_maps):
                s_ref[1 - half, c] = scores(nxt, c)
            vt = vt_slice(chunk_rows(cur))
            carry = tuple(absorb(s_ref[half, c], vt, c, *carry[c]) for c in range(n_maps))
        return carry

    init = tuple((jnp.full((1, tq), _NEG_BIG, _F32), jnp.zeros((1, tq), _F32)) for _ in range(n_maps))
    carry = lax.fori_loop(0, n_main // 2, body, init)
    vt_ctx = vt_slice(slice(seq, seq + ctx))
    carry = tuple(absorb(s_ref[0, c, tk - ctx:, :], vt_ctx, c, *carry[c]) for c in range(n_maps))
    return [l for (_, l) in carry]


def _gqa_attn_kernel(qt_ref, k_ref, vt_ref, o_ref, s_ref, *acc_refs, seq, ctx, tk):
    i = pl.program_id(1)
    tq = qt_ref.shape[1]
    n_main = jnp.where(i < seq // tq, seq // tk, 0)
    qt_list = [qt_ref[g * HEAD_DIM:(g + 1) * HEAD_DIM, :] for g in range(GQA_GROUP)]
    k_slices = [lambda rows: k_ref[rows, :]] * GQA_GROUP
    sums = _attend_t(qt_list, k_slices, lambda rows: vt_ref[:, rows], acc_refs, s_ref,
                     n_main=n_main, seq=seq, ctx=ctx, tk=tk)
    for g in range(GQA_GROUP):
        o_t = acc_refs[g][...] * (1.0 / sums[g])
        o_ref[:, g * HEAD_DIM:(g + 1) * HEAD_DIM] = o_t.T.astype(_BF16)


def _gqa_attention(q_t, k, v_t, seq, ctx):
    dq, rows = q_t.shape
    kvh = k.shape[1] // HEAD_DIM
    tq = ROW_GROUP
    qw = GQA_GROUP * HEAD_DIM
    tk = _key_chunk(seq, ctx)
    return pl.pallas_call(
        functools.partial(_gqa_attn_kernel, seq=seq, ctx=ctx, tk=tk),
        grid=(kvh, rows // tq),
        in_specs=[pl.BlockSpec((qw, tq), lambda g, i: (g, i)),
                  pl.BlockSpec((rows, HEAD_DIM), lambda g, i: (0, g)),
                  pl.BlockSpec((HEAD_DIM, rows), lambda g, i: (g, 0))],
        out_specs=pl.BlockSpec((tq, qw), lambda g, i: (i, g)),
        out_shape=jax.ShapeDtypeStruct((rows, dq), _BF16),
        scratch_shapes=[pltpu.VMEM((2, GQA_GROUP, tk, tq), _F32)] + [pltpu.VMEM((HEAD_DIM, tq), _F32)] * GQA_GROUP,
        compiler_params=_params(2),
        name="gqa_attention",
    )(q_t, k, v_t)


def _diff_attn_kernel(qt_ref, k_ref, vt_ref, lam_ref, sub_ref, o_ref, s_ref, *acc_refs, seq, ctx, tk, lambda_init):
    i = pl.program_id(1)
    tq = qt_ref.shape[1]
    n_main = jnp.where(i < seq // tq, seq // tk, 0)
    qt_list = [qt_ref[c * HEAD_DIM:(c + 1) * HEAD_DIM, :] for c in range(2)]
    k_slices = [functools.partial(lambda rows, c: k_ref[rows, c * HEAD_DIM:(c + 1) * HEAD_DIM], c=c)
                for c in range(2)]
    l0, l1 = _attend_t(qt_list, k_slices, lambda rows: vt_ref[:, rows], acc_refs, s_ref,
                       n_main=n_main, seq=seq, ctx=ctx, tk=tk)
    lp = lam_ref[...]
    lam = (jnp.exp(jnp.sum(lp[0:1, :] * lp[1:2, :], axis=1, keepdims=True))
           - jnp.exp(jnp.sum(lp[2:3, :] * lp[3:4, :], axis=1, keepdims=True)) + lambda_init)
    o_t = acc_refs[0][...] * (1.0 / l0) - lam * (acc_refs[1][...] * (1.0 / l1))
    o = o_t.T
    ms = jnp.mean(o * o, axis=-1, keepdims=True)
    o = o * lax.rsqrt(ms + EPS) * sub_ref[...] * (1.0 - lambda_init)
    o_ref[...] = o.astype(_BF16)


def _diff_attention(q_t, k, v_t, lam_params, subln, layer_j, seq, ctx, lambda_init):
    d, rows = q_t.shape
    hw = 2 * HEAD_DIM
    tq = ROW_GROUP
    tk = _key_chunk(seq, ctx)
    return pl.pallas_call(
        functools.partial(_diff_attn_kernel, seq=seq, ctx=ctx, tk=tk, lambda_init=lambda_init),
        grid=(d // hw, rows // tq),
        in_specs=[pl.BlockSpec((hw, tq), lambda h, i: (h, i)),
                  pl.BlockSpec((rows, hw), lambda h, i: (0, h)),
                  pl.BlockSpec((hw, rows), lambda h, i: (h, 0)),
                  pl.BlockSpec((None, 4, HEAD_DIM), lambda h, i: (layer_j, 0, 0)),
                  pl.BlockSpec((None, 1, hw), lambda h, i: (layer_j, 0, 0))],
        out_specs=pl.BlockSpec((tq, hw), lambda h, i: (i, h)),
        out_shape=jax.ShapeDtypeStruct((rows, d), _BF16),
        scratch_shapes=[pltpu.VMEM((2, 2, tk, tq), _F32)] + [pltpu.VMEM((hw, tq), _F32)] * 2,
        compiler_params=_params(2),
        name="diff_attention",
    )(q_t, k, v_t, lam_params, subln)


def _dft_cos_sin(n):
    kt = np.outer(np.arange(n), np.arange(n)) % n
    ang = 2.0 * np.pi * kt / n
    return np.cos(ang), np.sin(ang)


def _channel_dft(z, cs_ref, scale, store):
    p = z.shape[0] // 2
    gd = cs_ref.shape[1]
    zb = z.astype(_BF16)
    for g in range(z.shape[1] // gd):
        cols = slice(g * gd, (g + 1) * gd)
        zc = jnp.concatenate([zb[:p, cols], zb[p:, cols]], axis=1)
        y = jnp.dot(zc, cs_ref[...], preferred_element_type=_F32) * scale
        store(cols, y.astype(_BF16))


def _fft_stage1_kernel(a_ref, f1_ref, tw_ref, o_ref, *, d):
    n1 = a_ref.shape[0]
    for tt in range(a_ref.shape[1] // d):
        cols = slice(tt * d, (tt + 1) * d)
        a = jnp.dot(f1_ref[...], a_ref[:, cols], preferred_element_type=_F32)
        ar, ai = a[:n1, :], a[n1:, :]
        twr, twi = tw_ref[0, :, tt:tt + 1], tw_ref[1, :, tt:tt + 1]
        o_ref[0, :, cols] = (ar * twr - ai * twi).astype(_BF16)
        o_ref[1, :, cols] = (ar * twi + ai * twr).astype(_BF16)


def _fft_stage2_kernel(b_ref, g2_ref, cs_ref, y_ref, *, scale):
    d = b_ref.shape[3]
    for kk in range(b_ref.shape[1]):
        bcat = jnp.concatenate([b_ref[0, kk], b_ref[1, kk]], axis=0)
        z = jnp.dot(g2_ref[...], bcat, preferred_element_type=_F32)

        def store(cols, y, kk=kk):
            y_ref[:, kk * d + cols.start:kk * d + cols.stop] = y

        _channel_dft(z, cs_ref, scale, store)


def _ctx_dft_kernel(a_ref, fc_ref, cs_ref, y_ref, *, scale):
    z = jnp.dot(fc_ref[...], a_ref[...], preferred_element_type=_F32)

    def store(cols, y):
        y_ref[:, cols] = y

    _channel_dft(z, cs_ref, scale, store)


def _fourier_mix(h, seq, ctx):
    rows, d = h.shape
    gd = d // FNET_GROUPS
    n2 = FFT_MINOR
    n1 = seq // n2
    g2 = 16
    kb = 2
    assert seq % n2 == 0 and rows % n2 == 0 and n2 % g2 == 0 and n1 % kb == 0

    c1, s1 = _dft_cos_sin(n1)
    f1 = jnp.asarray(np.concatenate([c1, -s1], axis=0), _BF16)
    ang = 2.0 * np.pi * np.outer(np.arange(n1), np.arange(n2)) / seq
    tw = np.stack([np.cos(ang), -np.sin(ang)])
    tw = jnp.asarray(tw.reshape(2, n1, n2 // g2, g2).transpose(2, 0, 1, 3), _F32)
    c2, s2 = _dft_cos_sin(n2)
    g2m = jnp.asarray(np.block([[c2, s2], [-s2, c2]]), _BF16)
    cc, sc = _dft_cos_sin(gd)
    cs = jnp.asarray(np.concatenate([cc, sc], axis=0), _BF16)
    cx, sx = _dft_cos_sin(ctx)
    fc = jnp.asarray(np.concatenate([cx, -sx], axis=0), _BF16)

    b = pl.pallas_call(
        functools.partial(_fft_stage1_kernel, d=d),
        grid=(n2 // g2,),
        in_specs=[pl.BlockSpec((n1, g2 * d), lambda c: (0, c)),
                  pl.BlockSpec((2 * n1, n1), lambda c: (0, 0)),
                  pl.BlockSpec((None, 2, n1, g2), lambda c: (c, 0, 0, 0))],
        out_specs=pl.BlockSpec((2, n1, g2 * d), lambda c: (0, 0, c)),
        out_shape=jax.ShapeDtypeStruct((2, n1, n2 * d), _BF16),
        compiler_params=_params(1),
        name="fft_stage1",
    )(h.reshape(rows // n2, n2 * d), f1, tw)

    y_lat = pl.pallas_call(
        functools.partial(_fft_stage2_kernel, scale=1.0 / math.sqrt(seq * gd)),
        grid=(n1 // kb,),
        in_specs=[pl.BlockSpec((2, kb, n2, d), lambda k: (0, k, 0, 0)),
                  pl.BlockSpec((2 * n2, 2 * n2), lambda k: (0, 0)),
                  pl.BlockSpec((2 * gd, gd), lambda k: (0, 0))],
        out_specs=pl.BlockSpec((n2, kb * d), lambda k: (0, k)),
        out_shape=jax.ShapeDtypeStruct((n2, n1 * d), _BF16),
        compiler_params=_params(1),
        name="fft_stage2",
    )(b.reshape(2, n1, n2, d), g2m, cs)

    y_ctx = pl.pallas_call(
        functools.partial(_ctx_dft_kernel, scale=1.0 / math.sqrt(ctx * gd)),
        grid=(1,),
        in_specs=[pl.BlockSpec((ctx, d), lambda c: (seq // ctx, 0)),
                  pl.BlockSpec((2 * ctx, ctx), lambda c: (0, 0)),
                  pl.BlockSpec((2 * gd, gd), lambda c: (0, 0))],
        out_specs=pl.BlockSpec((ctx, d), lambda c: (0, 0)),
        out_shape=jax.ShapeDtypeStruct((ctx, d), _BF16),
        compiler_params=_params(1),
        name="ctx_dft",
    )(h, fc, cs)
    return jnp.concatenate([y_lat.reshape(seq, d), y_ctx], axis=0)


def _rope_tables(seq, ctx):
    quarter = HEAD_DIM // 4
    t = np.arange(seq)
    inv_freq = ROPE_THETA ** (-np.arange(quarter, dtype=np.float64) / quarter)
    ang_r = (t // GRID_W)[:, None] * inv_freq[None, :]
    ang_c = (t % GRID_W)[:, None] * inv_freq[None, :]
    cos = np.concatenate([np.cos(ang_r)] * 2 + [np.cos(ang_c)] * 2, axis=1)
    sin = np.concatenate([-np.sin(ang_r), np.sin(ang_r), -np.sin(ang_c), np.sin(ang_c)], axis=1)
    cos = np.concatenate([cos, np.ones((ctx, HEAD_DIM))], axis=0)
    sin = np.concatenate([sin, np.zeros((ctx, HEAD_DIM))], axis=0)
    return jnp.asarray(cos, _F32), jnp.asarray(sin, _F32)


def kernel(x, c, ctx, c_ctx, ada_w, ada_b, ln_g, ln_b, gqa_w_qkv, gqa_q_norm, gqa_k_norm, gqa_w_o,
           diff_w_qkv, diff_lambda, diff_subln, diff_w_o, fnet_w, fnet_b,
           ffn_w_up, ffn_conv_w, ffn_conv_b, ffn_w_down):
    batch, seq, d = x.shape
    n_ctx = ctx.shape[1]
    depth = ada_w.shape[0]
    assert batch == 1 and seq % ROW_GROUP == 0 and n_ctx % ROW_GROUP == 0 and seq % GRID_W == 0
    alpha = (2 * depth) ** 0.25
    q_scale = HEAD_DIM ** -0.5 * _LOG2E

    cvec = jnp.concatenate([c, c_ctx[None, :], jnp.zeros((6, d), _F32)], axis=0)
    mods = _ada_mods(cvec, ada_w, ada_b)
    xs = jnp.concatenate([x[0], ctx[0]], axis=0)
    rows = seq + n_ctx
    rope_tabs = _rope_tables(seq, n_ctx)
    ln_g2 = ln_g.reshape(depth * 2, 1, d)
    ln_b2 = ln_b.reshape(depth * 2, 1, d)
    conv_b3 = ffn_conv_b.reshape(depth, 1, -1)
    gqa_q_norm3 = gqa_q_norm.reshape(-1, 1, HEAD_DIM)
    gqa_k_norm3 = gqa_k_norm.reshape(-1, 1, HEAD_DIM)
    diff_subln3 = diff_subln.reshape(-1, 1, 2 * HEAD_DIM)
    fnet_b3 = fnet_b.reshape(-1, 1, d)

    (h,) = _ln_mod(xs, rows, seq, None, None, 0, mods, 0, 0, do_ln=False, emit_h=True)
    counts = [0, 0, 0]
    for i in range(depth):
        last = i == depth - 1
        kind = i % N_MIXERS
        j = counts[kind]
        counts[kind] += 1
        if kind == 0:
            n_q = d
            n_kv = d // GQA_GROUP
            q_t = _head_proj(h, gqa_w_qkv, j, 0, n_q, transpose_out=True, rope_tabs=rope_tabs,
                             gain=gqa_q_norm3, out_scale=q_scale)
            k = _head_proj(h, gqa_w_qkv, j, n_q, n_kv, transpose_out=False, rope_tabs=rope_tabs, gain=gqa_k_norm3)
            v_t = _head_proj(h, gqa_w_qkv, j, n_q + n_kv, n_kv, transpose_out=True)
            o = _gqa_attention(q_t, k, v_t, seq, n_ctx)
            z = _resid_proj(o, gqa_w_o, j, None, xs, mods, i, 2, seq, alpha)
        elif kind == 1:
            lambda_init = 0.8 - 0.6 * math.exp(-0.3 * i)
            q_t = _head_proj(h, diff_w_qkv, j, 0, d, transpose_out=True, rope_tabs=rope_tabs, out_scale=q_scale)
            k = _head_proj(h, diff_w_qkv, j, d, d, transpose_out=False, rope_tabs=rope_tabs)
            v_t = _head_proj(h, diff_w_qkv, j, 2 * d, d, transpose_out=True)
            o = _diff_attention(q_t, k, v_t, diff_lambda, diff_subln3, j, seq, n_ctx, lambda_init)
            z = _resid_proj(o, diff_w_o, j, None, xs, mods, i, 2, seq, alpha)
        else:
            y = _fourier_mix(h, seq, n_ctx)
            z = _resid_proj(y, fnet_w, j, fnet_b3, xs, mods, i, 2, seq, alpha)
        xs, h = _ln_mod(z, rows, seq, ln_g2, ln_b2, 2 * i, mods, i, 3, do_ln=True, emit_h=True)
        a = _ffn_up(h, ffn_w_up, ffn_conv_w, conv_b3, i, seq)
        z = _resid_proj(a, ffn_w_down, i, None, xs, mods, i, 5, seq, alpha)
        if last:
            (xs,) = _ln_mod(z, seq, seq, ln_g2, ln_b2, 2 * i + 1, mods, 0, 0, do_ln=True, emit_h=False)
        else:
            xs, h = _ln_mod(z, rows, seq, ln_g2, ln_b2, 2 * i + 1, mods, i + 1, 0, do_ln=True, emit_h=True)
    return xs[None]
```

```python
import functools
import math

import numpy as np
import jax
import jax.numpy as jnp
from jax import lax
from jax.experimental import pallas as pl
from jax.experimental.pallas import tpu as pltpu

HEAD_DIM = 128
GQA_GROUP = 4
GRID_W = 64
N_MIXERS = 3
FNET_GROUPS = 4
ROPE_THETA = 10000.0
EPS = 1e-6
ROW_GROUP = 256
FFT_MINOR = 128
BF16_SUBLANES = 16
LN_ROW_CHUNK = 32
V7X_VMEM_LIMIT_BYTES = 56 * 1024 * 1024

_F32 = jnp.float32
_BF16 = jnp.bfloat16


def _params(n_grid_axes):
    return pltpu.CompilerParams(dimension_semantics=("arbitrary",) * n_grid_axes,
                                vmem_limit_bytes=V7X_VMEM_LIMIT_BYTES)


def _largest_tile(n, unit, cap):
    best = None
    for t in range(unit, min(n, cap) + 1, unit):
        if n % t == 0:
            best = t
    assert best is not None, (n, unit, cap)
    return best


def _silu(x):
    return x * (1.0 / (1.0 + jnp.exp(-x)))


def _ada_kernel(cv_ref, w_ref, b_ref, o_ref):
    s = _silu(cv_ref[...]).astype(_BF16)
    acc = jnp.dot(s, w_ref[...].astype(_BF16), preferred_element_type=_F32)
    o_ref[...] = acc + b_ref[...]


def _ada_mods(cvec, ada_w, ada_b):
    depth, d, n = ada_w.shape
    tn = _largest_tile(n, 128, 1024)
    return pl.pallas_call(
        _ada_kernel,
        grid=(depth, n // tn),
        in_specs=[pl.BlockSpec((8, d), lambda l, j: (0, 0)),
                  pl.BlockSpec((None, d, tn), lambda l, j: (l, 0, j)),
                  pl.BlockSpec((None, 1, tn), lambda l, j: (l, 0, j))],
        out_specs=pl.BlockSpec((None, 8, tn), lambda l, j: (l, 0, j)),
        out_shape=jax.ShapeDtypeStruct((depth, 8, n), _F32),
        compiler_params=_params(2),
        name="ada_mods",
    )(cvec, ada_w, ada_b.reshape(depth, 1, n))


def _ln_mod_kernel(*refs, do_ln, emit_h, n_sub, n_lat_groups):
    it = iter(refs)
    z_ref = next(it)
    g_ref = next(it) if do_ln else None
    b_ref = next(it) if do_ln else None
    sh_ref = next(it) if emit_h else None
    sc_ref = next(it) if emit_h else None
    x_ref = next(it) if do_ln else None
    h_ref = next(it) if emit_h else None
    i = pl.program_id(0)
    for s in range(n_sub):
        rows = slice(s * ROW_GROUP, (s + 1) * ROW_GROUP)
        x = z_ref[rows, :]
        if do_ln:
            mu = jnp.mean(x, axis=-1, keepdims=True)
            xc = x - mu
            var = jnp.mean(xc * xc, axis=-1, keepdims=True)
            x = xc * lax.rsqrt(var + EPS) * g_ref[...] + b_ref[...]
            x_ref[rows, :] = x
        if emit_h:
            is_ctx = (i * n_sub + s) >= n_lat_groups
            sh = jnp.where(is_ctx, sh_ref[1:2, :], sh_ref[0:1, :])
            sc = jnp.where(is_ctx, sc_ref[1:2, :], sc_ref[0:1, :])
            h_ref[rows, :] = (x * (1.0 + sc) + sh).astype(_BF16)


def _ln_mod(z, n_rows, seq, ln_g2, ln_b2, ln_idx, mods, mod_layer, shift_idx, *, do_ln, emit_h):
    d = z.shape[1]
    tm = _largest_tile(n_rows, ROW_GROUP, 1024)
    in_specs = [pl.BlockSpec((tm, d), lambda i: (i, 0))]
    args = [z]
    if do_ln:
        in_specs += [pl.BlockSpec((None, 1, d), lambda i: (ln_idx, 0, 0))] * 2
        args += [ln_g2, ln_b2]
    if emit_h:
        in_specs += [pl.BlockSpec((None, 8, d), lambda i: (mod_layer, 0, shift_idx)),
                     pl.BlockSpec((None, 8, d), lambda i: (mod_layer, 0, shift_idx + 1))]
        args += [mods, mods]
    out_specs, out_shape = [], []
    if do_ln:
        out_specs.append(pl.BlockSpec((tm, d), lambda i: (i, 0)))
        out_shape.append(jax.ShapeDtypeStruct((n_rows, d), _F32))
    if emit_h:
        out_specs.append(pl.BlockSpec((tm, d), lambda i: (i, 0)))
        out_shape.append(jax.ShapeDtypeStruct((n_rows, d), _BF16))
    outs = pl.pallas_call(
        functools.partial(_ln_mod_kernel, do_ln=do_ln, emit_h=emit_h, n_sub=tm // ROW_GROUP,
                          n_lat_groups=seq // ROW_GROUP),
        grid=(n_rows // tm,),
        in_specs=in_specs, out_specs=out_specs, out_shape=out_shape,
        compiler_params=_params(1),
        name="ln_mod",
    )(*args)
    return outs


def _rope(y, cos, sin_signed):
    lane = lax.broadcasted_iota(jnp.int32, y.shape, 1)
    partner = jnp.where((lane % 64) < 32, pltpu.roll(y, 96, 1), pltpu.roll(y, 32, 1))
    return y * cos + partner * sin_signed


def _lagged_steps(i, n_tiles, produce, consume):
    @pl.when(jnp.logical_and(i < n_tiles, i % 2 == 0))
    def _():
        consume(1)
        produce(0)

    @pl.when(jnp.logical_and(i < n_tiles, i % 2 == 1))
    def _():
        consume(0)
        produce(1)

    @pl.when(i == n_tiles)
    def _():
        consume((n_tiles - 1) % 2)


def _lagged_index_maps(n_tiles):
    return (lambda i: jnp.minimum(i, n_tiles - 1)), (lambda i: jnp.maximum(i - 1, 0))


def _head_proj_kernel(*refs, n_tiles, rope, transpose_out, has_norm, out_scale):
    it = iter(refs)
    h_ref, w_ref = next(it), next(it)
    gain_ref = next(it) if has_norm else None
    cos_ref = next(it) if rope else None
    sin_ref = next(it) if rope else None
    o_ref, wbf_ref, raw_ref = next(it), next(it), next(it)
    i = pl.program_id(1)

    @pl.when(i == 0)
    def _():
        wbf_ref[...] = w_ref[...].astype(_BF16)
        raw_ref[1] = jnp.zeros(raw_ref.shape[1:], _F32)

    def produce(slot):
        raw_ref[slot] = jnp.dot(h_ref[...], wbf_ref[...], preferred_element_type=_F32)

    def consume(slot):
        for hh in range(raw_ref.shape[2] // HEAD_DIM):
            cols = slice(hh * HEAD_DIM, (hh + 1) * HEAD_DIM)
            y = raw_ref[slot, :, cols]
            if has_norm:
                ms = jnp.mean(y * y, axis=-1, keepdims=True)
                y = y * lax.rsqrt(ms + EPS) * gain_ref[...]
            if rope:
                y = _rope(y, cos_ref[...], sin_ref[...])
            if out_scale != 1.0:
                y = y * out_scale
            if transpose_out:
                o_ref[cols, :] = y.T.astype(_BF16)
            else:
                o_ref[:, cols] = y.astype(_BF16)

    _lagged_steps(i, n_tiles, produce, consume)


def _head_proj(h, w, w_layer, col0, n_cols, *, transpose_out, rope_tabs=None, gain=None, out_scale=1.0):
    rows, d = h.shape
    tm = _largest_tile(rows, ROW_GROUP, 1024)
    tn = _largest_tile(math.gcd(n_cols, col0) if col0 else n_cols, HEAD_DIM, 1024)
    j0 = col0 // tn
    n_tiles = rows // tm
    produced, consumed = _lagged_index_maps(n_tiles)
    in_specs = [pl.BlockSpec((tm, d), lambda j, i: (produced(i), 0)),
                pl.BlockSpec((None, d, tn), lambda j, i: (w_layer, 0, j0 + j))]
    args = [h, w]
    if gain is not None:
        in_specs.append(pl.BlockSpec((None, 1, HEAD_DIM), lambda j, i: (w_layer, 0, 0)))
        args.append(gain)
    if rope_tabs is not None:
        in_specs += [pl.BlockSpec((tm, HEAD_DIM), lambda j, i: (consumed(i), 0))] * 2
        args += list(rope_tabs)
    if transpose_out:
        out_spec = pl.BlockSpec((tn, tm), lambda j, i: (j, consumed(i)))
        out_shape = jax.ShapeDtypeStruct((n_cols, rows), _BF16)
    else:
        out_spec = pl.BlockSpec((tm, tn), lambda j, i: (consumed(i), j))
        out_shape = jax.ShapeDtypeStruct((rows, n_cols), _BF16)
    return pl.pallas_call(
        functools.partial(_head_proj_kernel, n_tiles=n_tiles, rope=rope_tabs is not None,
                          transpose_out=transpose_out, has_norm=gain is not None, out_scale=out_scale),
        grid=(n_cols // tn, n_tiles + 1),
        in_specs=in_specs, out_specs=out_spec, out_shape=out_shape,
        scratch_shapes=[pltpu.VMEM((d, tn), _BF16), pltpu.VMEM((2, tm, tn), _F32)],
        compiler_params=_params(2),
        name="head_proj",
    )(*args)


def _cast_kernel(w_ref, o_ref):
    o_ref[...] = w_ref[...].astype(_BF16)


def _weight_bf16(w, w_layer):
    _, k, n = w.shape
    tk = _largest_tile(k, BF16_SUBLANES, 512)
    return pl.pallas_call(
        _cast_kernel,
        grid=(k // tk,),
        in_specs=[pl.BlockSpec((None, tk, n), lambda r: (w_layer, r, 0))],
        out_specs=pl.BlockSpec((tk, n), lambda r: (r, 0)),
        out_shape=jax.ShapeDtypeStruct((k, n), _BF16),
        compiler_params=_params(1),
        name="weight_bf16",
    )(w)


def _resid_ln_kernel(*refs, n_tiles, alpha, has_bias, emit_h, n_lat_groups):
    it = iter(refs)
    a_ref, w_ref = next(it), next(it)
    b_ref = next(it) if has_bias else None
    x_ref, gate_ref, g_ref, be_ref = next(it), next(it), next(it), next(it)
    sh_ref = next(it) if emit_h else None
    sc_ref = next(it) if emit_h else None
    xo_ref = next(it)
    h_ref = next(it) if emit_h else None
    raw_ref = next(it)
    i = pl.program_id(0)
    tm = raw_ref.shape[1]

    @pl.when(i == 0)
    def _():
        raw_ref[1] = jnp.zeros(raw_ref.shape[1:], _F32)

    def produce(slot):
        raw_ref[slot] = jnp.dot(a_ref[...], w_ref[...], preferred_element_type=_F32)

    def consume(slot):
        is_ctx = (i - 1) >= n_lat_groups
        gate = jnp.where(is_ctx, gate_ref[1:2, :], gate_ref[0:1, :])
        if emit_h:
            sh = jnp.where(is_ctx, sh_ref[1:2, :], sh_ref[0:1, :])
            sc1 = 1.0 + jnp.where(is_ctx, sc_ref[1:2, :], sc_ref[0:1, :])
        for r0 in range(0, tm, LN_ROW_CHUNK):
            rows = slice(r0, r0 + LN_ROW_CHUNK)
            acc = raw_ref[slot, rows, :]
            if has_bias:
                acc = acc + b_ref[...]
            z = alpha * x_ref[rows, :] + gate * acc
            mu = jnp.mean(z, axis=-1, keepdims=True)
            zc = z - mu
            var = jnp.mean(zc * zc, axis=-1, keepdims=True)
            x = zc * lax.rsqrt(var + EPS) * g_ref[...] + be_ref[...]
            xo_ref[rows, :] = x
            if emit_h:
                h_ref[rows, :] = (x * sc1 + sh).astype(_BF16)

    _lagged_steps(i, n_tiles, produce, consume)


def _resid_ln(a, w_bf, bias, bias_layer, xs, n_rows, seq, mods, mod_layer, gate_idx, ln_g2, ln_b2, ln_idx,
              next_layer, next_shift_idx, alpha, *, emit_h):
    k, d = w_bf.shape
    tm = ROW_GROUP
    n_tiles = n_rows // tm
    produced, consumed = _lagged_index_maps(n_tiles)
    in_specs = [pl.BlockSpec((tm, k), lambda i: (produced(i), 0)),
                pl.BlockSpec((k, d), lambda i: (0, 0), pipeline_mode=pl.Buffered(1))]
    args = [a, w_bf]
    if bias is not None:
        in_specs.append(pl.BlockSpec((None, 1, d), lambda i: (bias_layer, 0, 0)))
        args.append(bias)
    in_specs += [pl.BlockSpec((tm, d), lambda i: (consumed(i), 0)),
                 pl.BlockSpec((None, 8, d), lambda i: (mod_layer, 0, gate_idx)),
                 pl.BlockSpec((None, 1, d), lambda i: (ln_idx, 0, 0)),
                 pl.BlockSpec((None, 1, d), lambda i: (ln_idx, 0, 0))]
    args += [xs, mods, ln_g2, ln_b2]
    out_specs = [pl.BlockSpec((tm, d), lambda i: (consumed(i), 0))]
    out_shape = [jax.ShapeDtypeStruct((n_rows, d), _F32)]
    if emit_h:
        in_specs += [pl.BlockSpec((None, 8, d), lambda i: (next_layer, 0, next_shift_idx)),
                     pl.BlockSpec((None, 8, d), lambda i: (next_layer, 0, next_shift_idx + 1))]
        args += [mods, mods]
        out_specs.append(pl.BlockSpec((tm, d), lambda i: (consumed(i), 0)))
        out_shape.append(jax.ShapeDtypeStruct((n_rows, d), _BF16))
    return pl.pallas_call(
        functools.partial(_resid_ln_kernel, n_tiles=n_tiles, alpha=alpha, has_bias=bias is not None,
                          emit_h=emit_h, n_lat_groups=seq // ROW_GROUP),
        grid=(n_tiles + 1,),
        in_specs=in_specs, out_specs=out_specs, out_shape=out_shape,
        scratch_shapes=[pltpu.VMEM((2, tm, d), _F32)],
        compiler_params=_params(1),
        name="resid_ln",
    )(*args)


def _ffn_up_kernel(hp_ref, hm_ref, hn_ref, wv_ref, wg_ref, cwv_ref, cwg_ref, cbv_ref, cbg_ref, o_ref,
                   wv_bf, wg_bf, hs_ref, *, seq, rows):
    i = pl.program_id(1)
    tm = hm_ref.shape[0]
    halo = BF16_SUBLANES

    @pl.when(i == 0)
    def _():
        wv_bf[...] = wv_ref[...].astype(_BF16)
        wg_bf[...] = wg_ref[...].astype(_BF16)

    hs_ref[0:halo, :] = hp_ref[...]
    hs_ref[halo:halo + tm, :] = hm_ref[...]
    hs_ref[halo + tm:, :] = hn_ref[...]
    hs = hs_ref[...]
    r = i * tm + lax.broadcasted_iota(jnp.int32, (tm, 1), 0)
    keep_prev = jnp.where(r == 0, 0.0, 1.0) * jnp.where(r == seq, 0.0, 1.0)
    keep_next = jnp.where(r == seq - 1, 0.0, 1.0) * jnp.where(r == rows - 1, 0.0, 1.0)

    def conv_branch(w_bf, cw_ref, cb_ref):
        u = jnp.dot(hs, w_bf[...], preferred_element_type=_F32)
        cw = cw_ref[...]
        return (cw[1:2, :] * u[halo:halo + tm, :]
                + keep_prev * (cw[0:1, :] * u[halo - 1:halo - 1 + tm, :])
                + keep_next * (cw[2:3, :] * u[halo + 1:halo + 1 + tm, :])
                + cb_ref[...])

    val = conv_branch(wv_bf, cwv_ref, cbv_ref)
    gate = conv_branch(wg_bf, cwg_ref, cbg_ref)
    o_ref[...] = (_silu(gate) * val).astype(_BF16)


def _ffn_up(h, w_up, conv_w, conv_b3, layer, seq):
    rows, d = h.shape
    d_ff = w_up.shape[2] // 2
    tm = _largest_tile(rows, ROW_GROUP, 1024)
    tf = _largest_tile(d_ff, 128, 512)
    nf = d_ff // tf
    halo = BF16_SUBLANES
    per_tile = tm // halo
    last_halo_blk = rows // halo - 1
    return pl.pallas_call(
        functools.partial(_ffn_up_kernel, seq=seq, rows=rows),
        grid=(nf, rows // tm),
        in_specs=[pl.BlockSpec((halo, d), lambda j, i: (jnp.maximum(i * per_tile - 1, 0), 0)),
                  pl.BlockSpec((tm, d), lambda j, i: (i, 0)),
                  pl.BlockSpec((halo, d), lambda j, i: (jnp.minimum((i + 1) * per_tile, last_halo_blk), 0)),
                  pl.BlockSpec((None, d, tf), lambda j, i: (layer, 0, j)),
                  pl.BlockSpec((None, d, tf), lambda j, i: (layer, 0, nf + j)),
                  pl.BlockSpec((None, 3, tf), lambda j, i: (layer, 0, j)),
                  pl.BlockSpec((None, 3, tf), lambda j, i: (layer, 0, nf + j)),
                  pl.BlockSpec((None, 1, tf), lambda j, i: (layer, 0, j)),
                  pl.BlockSpec((None, 1, tf), lambda j, i: (layer, 0, nf + j))],
        out_specs=pl.BlockSpec((tm, tf), lambda j, i: (i, j)),
        out_shape=jax.ShapeDtypeStruct((rows, d_ff), _BF16),
        scratch_shapes=[pltpu.VMEM((d, tf), _BF16), pltpu.VMEM((d, tf), _BF16),
                        pltpu.VMEM((tm + 2 * halo, d), _BF16)],
        compiler_params=_params(2),
        name="ffn_up",
    )(h, h, h, w_up, w_up, conv_w, conv_w, conv_b3, conv_b3)


_NEG_BIG = -1e30
_LOG2E = math.log2(math.e)
ATTN_KEY_CHUNK_CAP = 1024


def _key_chunk(seq, ctx):
    tk = max(t for t in range(128, min(seq // 2, ATTN_KEY_CHUNK_CAP) + 1, 128)
             if seq % t == 0 and (seq // t) % 2 == 0)
    assert tk >= ctx and ctx % 128 == 0, (tk, ctx)
    return tk


def _attend_t(qt_list, k_slices, vt_slice, acc_refs, s_ref, *, n_main, seq, ctx, tk):
    n_maps = len(qt_list)
    tq = qt_list[0].shape[1]

    def scores(rows, c):
        return jnp.dot(k_slices[c](rows), qt_list[c], preferred_element_type=_F32)

    def absorb(s, vt, c, m, l):
        m_new = jnp.maximum(m, jnp.max(s, axis=0, keepdims=True))
        p = jnp.exp2(s - m_new)
        alpha = jnp.exp2(m - m_new)
        acc_refs[c][...] = alpha * acc_refs[c][...] + jnp.dot(vt, p.astype(_BF16), preferred_element_type=_F32)
        return m_new, alpha * l + jnp.sum(p, axis=0, keepdims=True)

    tail_start = seq + ctx - tk

    def chunk_rows(n):
        start = jnp.where(n < n_main, n * tk, tail_start)
        return pl.ds(pl.multiple_of(start, math.gcd(tk, tail_start)), tk)

    for acc_ref in acc_refs:
        acc_ref[...] = jnp.zeros(acc_ref.shape, _F32)
    for c in range(n_maps):
        s_ref[0, c] = scores(chunk_rows(0), c)

    def body(j, carry):
        for half in range(2):
            cur = 2 * j + half
            nxt = chunk_rows(cur + 1)
            for c in range(n_maps):
                s_ref[1 - half, c] = scores(nxt, c)
            vt = vt_slice(chunk_rows(cur))
            carry = tuple(absorb(s_ref[half, c], vt, c, *carry[c]) for c in range(n_maps))
        return carry

    init = tuple((jnp.full((1, tq), _NEG_BIG, _F32), jnp.zeros((1, tq), _F32)) for _ in range(n_maps))
    carry = lax.fori_loop(0, n_main // 2, body, init)
    vt_ctx = vt_slice(slice(seq, seq + ctx))
    carry = tuple(absorb(s_ref[0, c, tk - ctx:, :], vt_ctx, c, *carry[c]) for c in range(n_maps))
    return [l for (_, l) in carry]


def _gqa_attn_kernel(qt_ref, k_ref, vt_ref, o_ref, s_ref, *acc_refs, seq, ctx, tk):
    i = pl.program_id(1)
    tq = qt_ref.shape[1]
    n_main = jnp.where(i < seq // tq, seq // tk, 0)
    qt_list = [qt_ref[g * HEAD_DIM:(g + 1) * HEAD_DIM, :] for g in range(GQA_GROUP)]
    k_slices = [lambda rows: k_ref[rows, :]] * GQA_GROUP
    sums = _attend_t(qt_list, k_slices, lambda rows: vt_ref[:, rows], acc_refs, s_ref,
                     n_main=n_main, seq=seq, ctx=ctx, tk=tk)
    for g in range(GQA_GROUP):
        o_t = acc_refs[g][...] * (1.0 / sums[g])
        o_ref[:, g * HEAD_DIM:(g + 1) * HEAD_DIM] = o_t.T.astype(_BF16)


def _gqa_attention(q_t, k, v_t, seq, ctx):
    dq, rows = q_t.shape
    kvh = k.shape[1] // HEAD_DIM
    tq = ROW_GROUP
    qw = GQA_GROUP * HEAD_DIM
    tk = _key_chunk(seq, ctx)
    return pl.pallas_call(
        functools.partial(_gqa_attn_kernel, seq=seq, ctx=ctx, tk=tk),
        grid=(kvh, rows // tq),
        in_specs=[pl.BlockSpec((qw, tq), lambda g, i: (g, i)),
                  pl.BlockSpec((rows, HEAD_DIM), lambda g, i: (0, g)),
                  pl.BlockSpec((HEAD_DIM, rows), lambda g, i: (g, 0))],
        out_specs=pl.BlockSpec((tq, qw), lambda g, i: (i, g)),
        out_shape=jax.ShapeDtypeStruct((rows, dq), _BF16),
        scratch_shapes=[pltpu.VMEM((2, GQA_GROUP, tk, tq), _F32)] + [pltpu.VMEM((HEAD_DIM, tq), _F32)] * GQA_GROUP,
        compiler_params=_params(2),
        name="gqa_attention",
    )(q_t, k, v_t)


def _diff_attn_kernel(qt_ref, k_ref, vt_ref, lam_ref, sub_ref, o_ref, s_ref, *acc_refs, seq, ctx, tk, lambda_init):
    i = pl.program_id(1)
    tq = qt_ref.shape[1]
    n_main = jnp.where(i < seq // tq, seq // tk, 0)
    qt_list = [qt_ref[c * HEAD_DIM:(c + 1) * HEAD_DIM, :] for c in range(2)]
    k_slices = [functools.partial(lambda rows, c: k_ref[rows, c * HEAD_DIM:(c + 1) * HEAD_DIM], c=c)
                for c in range(2)]
    l0, l1 = _attend_t(qt_list, k_slices, lambda rows: vt_ref[:, rows], acc_refs, s_ref,
                       n_main=n_main, seq=seq, ctx=ctx, tk=tk)
    lp = lam_ref[...]
    lam = (jnp.exp(jnp.sum(lp[0:1, :] * lp[1:2, :], axis=1, keepdims=True))
           - jnp.exp(jnp.sum(lp[2:3, :] * lp[3:4, :], axis=1, keepdims=True)) + lambda_init)
    o_t = acc_refs[0][...] * (1.0 / l0) - lam * (acc_refs[1][...] * (1.0 / l1))
    o = o_t.T
    ms = jnp.mean(o * o, axis=-1, keepdims=True)
    o = o * lax.rsqrt(ms + EPS) * sub_ref[...] * (1.0 - lambda_init)
    o_ref[...] = o.astype(_BF16)


def _diff_attention(q_t, k, v_t, lam_params, subln, layer_j, seq, ctx, lambda_init):
    d, rows = q_t.shape
    hw = 2 * HEAD_DIM
    tq = ROW_GROUP
    tk = _key_chunk(seq, ctx)
    return pl.pallas_call(
        functools.partial(_diff_attn_kernel, seq=seq, ctx=ctx, tk=tk, lambda_init=lambda_init),
        grid=(d // hw, rows // tq),
        in_specs=[pl.BlockSpec((hw, tq), lambda h, i: (h, i)),
                  pl.BlockSpec((rows, hw), lambda h, i: (0, h)),
                  pl.BlockSpec((hw, rows), lambda h, i: (h, 0)),
                  pl.BlockSpec((None, 4, HEAD_DIM), lambda h, i: (layer_j, 0, 0)),
                  pl.BlockSpec((None, 1, hw), lambda h, i: (layer_j, 0, 0))],
        out_specs=pl.BlockSpec((tq, hw), lambda h, i: (i, h)),
        out_shape=jax.ShapeDtypeStruct((rows, d), _BF16),
        scratch_shapes=[pltpu.VMEM((2, 2, tk, tq), _F32)] + [pltpu.VMEM((hw, tq), _F32)] * 2,
        compiler_params=_params(2),
        name="diff_attention",
    )(q_t, k, v_t, lam_params, subln)


def _dft_cos_sin(n):
    kt = np.outer(np.arange(n), np.arange(n)) % n
    ang = 2.0 * np.pi * kt / n
    return np.cos(ang), np.sin(ang)


def _channel_dft(z, cs_ref, scale, store):
    p = z.shape[0] // 2
    gd = cs_ref.shape[1]
    zb = z.astype(_BF16)
    for g in range(z.shape[1] // gd):
        cols = slice(g * gd, (g + 1) * gd)
        zc = jnp.concatenate([zb[:p, cols], zb[p:, cols]], axis=1)
        y = jnp.dot(zc, cs_ref[...], preferred_element_type=_F32) * scale
        store(cols, y.astype(_BF16))


def _fft_stage1_kernel(a_ref, f1_ref, tw_ref, o_ref, *, d):
    n1 = a_ref.shape[0]
    for tt in range(a_ref.shape[1] // d):
        cols = slice(tt * d, (tt + 1) * d)
        a = jnp.dot(f1_ref[...], a_ref[:, cols], preferred_element_type=_F32)
        ar, ai = a[:n1, :], a[n1:, :]
        twr, twi = tw_ref[0, :, tt:tt + 1], tw_ref[1, :, tt:tt + 1]
        o_ref[0, :, cols] = (ar * twr - ai * twi).astype(_BF16)
        o_ref[1, :, cols] = (ar * twi + ai * twr).astype(_BF16)


def _fft_stage2_kernel(b_ref, g2_ref, cs_ref, y_ref, *, scale):
    d = b_ref.shape[3]
    for kk in range(b_ref.shape[1]):
        bcat = jnp.concatenate([b_ref[0, kk], b_ref[1, kk]], axis=0)
        z = jnp.dot(g2_ref[...], bcat, preferred_element_type=_F32)

        def store(cols, y, kk=kk):
            y_ref[:, kk * d + cols.start:kk * d + cols.stop] = y

        _channel_dft(z, cs_ref, scale, store)


def _ctx_dft_kernel(a_ref, fc_ref, cs_ref, y_ref, *, scale):
    z = jnp.dot(fc_ref[...], a_ref[...], preferred_element_type=_F32)

    def store(cols, y):
        y_ref[:, cols] = y

    _channel_dft(z, cs_ref, scale, store)


def _fourier_mix(h, seq, ctx):
    rows, d = h.shape
    gd = d // FNET_GROUPS
    n2 = FFT_MINOR
    n1 = seq // n2
    g2 = 16
    kb = 2
    assert seq % n2 == 0 and rows % n2 == 0 and n2 % g2 == 0 and n1 % kb == 0

    c1, s1 = _dft_cos_sin(n1)
    f1 = jnp.asarray(np.concatenate([c1, -s1], axis=0), _BF16)
    ang = 2.0 * np.pi * np.outer(np.arange(n1), np.arange(n2)) / seq
    tw = np.stack([np.cos(ang), -np.sin(ang)])
    tw = jnp.asarray(tw.reshape(2, n1, n2 // g2, g2).transpose(2, 0, 1, 3), _F32)
    c2, s2 = _dft_cos_sin(n2)
    g2m = jnp.asarray(np.block([[c2, s2], [-s2, c2]]), _BF16)
    cc, sc = _dft_cos_sin(gd)
    cs = jnp.asarray(np.concatenate([cc, sc], axis=0), _BF16)
    cx, sx = _dft_cos_sin(ctx)
    fc = jnp.asarray(np.concatenate([cx, -sx], axis=0), _BF16)

    b = pl.pallas_call(
        functools.partial(_fft_stage1_kernel, d=d),
        grid=(n2 // g2,),
        in_specs=[pl.BlockSpec((n1, g2 * d), lambda c: (0, c)),
                  pl.BlockSpec((2 * n1, n1), lambda c: (0, 0)),
                  pl.BlockSpec((None, 2, n1, g2), lambda c: (c, 0, 0, 0))],
        out_specs=pl.BlockSpec((2, n1, g2 * d), lambda c: (0, 0, c)),
        out_shape=jax.ShapeDtypeStruct((2, n1, n2 * d), _BF16),
        compiler_params=_params(1),
        name="fft_stage1",
    )(h.reshape(rows // n2, n2 * d), f1, tw)

    y_lat = pl.pallas_call(
        functools.partial(_fft_stage2_kernel, scale=1.0 / math.sqrt(seq * gd)),
        grid=(n1 // kb,),
        in_specs=[pl.BlockSpec((2, kb, n2, d), lambda k: (0, k, 0, 0)),
                  pl.BlockSpec((2 * n2, 2 * n2), lambda k: (0, 0)),
                  pl.BlockSpec((2 * gd, gd), lambda k: (0, 0))],
        out_specs=pl.BlockSpec((n2, kb * d), lambda k: (0, k)),
        out_shape=jax.ShapeDtypeStruct((n2, n1 * d), _BF16),
        compiler_params=_params(1),
        name="fft_stage2",
    )(b.reshape(2, n1, n2, d), g2m, cs)

    y_ctx = pl.pallas_call(
        functools.partial(_ctx_dft_kernel, scale=1.0 / math.sqrt(ctx * gd)),
        grid=(1,),
        in_specs=[pl.BlockSpec((ctx, d), lambda c: (seq // ctx, 0)),
                  pl.BlockSpec((2 * ctx, ctx), lambda c: (0, 0)),
                  pl.BlockSpec((2 * gd, gd), lambda c: (0, 0))],
        out_specs=pl.BlockSpec((ctx, d), lambda c: (0, 0)),
        out_shape=jax.ShapeDtypeStruct((ctx, d), _BF16),
        compiler_params=_params(1),
        name="ctx_dft",
    )(h, fc, cs)
    return jnp.concatenate([y_lat.reshape(seq, d), y_ctx], axis=0)


def _rope_tables(seq, ctx):
    quarter = HEAD_DIM // 4
    t = np.arange(seq)
    inv_freq = ROPE_THETA ** (-np.arange(quarter, dtype=np.float64) / quarter)
    ang_r = (t // GRID_W)[:, None] * inv_freq[None, :]
    ang_c = (t % GRID_W)[:, None] * inv_freq[None, :]
    cos = np.concatenate([np.cos(ang_r)] * 2 + [np.cos(ang_c)] * 2, axis=1)
    sin = np.concatenate([-np.sin(ang_r), np.sin(ang_r), -np.sin(ang_c), np.sin(ang_c)], axis=1)
    cos = np.concatenate([cos, np.ones((ctx, HEAD_DIM))], axis=0)
    sin = np.concatenate([sin, np.zeros((ctx, HEAD_DIM))], axis=0)
    return jnp.asarray(cos, _F32), jnp.asarray(sin, _F32)


def kernel(x, c, ctx, c_ctx, ada_w, ada_b, ln_g, ln_b, gqa_w_qkv, gqa_q_norm, gqa_k_norm, gqa_w_o,
           diff_w_qkv, diff_lambda, diff_subln, diff_w_o, fnet_w, fnet_b,
           ffn_w_up, ffn_conv_w, ffn_conv_b, ffn_w_down):
    batch, seq, d = x.shape
    n_ctx = ctx.shape[1]
    depth = ada_w.shape[0]
    assert batch == 1 and seq % ROW_GROUP == 0 and n_ctx % ROW_GROUP == 0 and seq % GRID_W == 0
    alpha = (2 * depth) ** 0.25
    q_scale = HEAD_DIM ** -0.5 * _LOG2E

    cvec = jnp.concatenate([c, c_ctx[None, :], jnp.zeros((6, d), _F32)], axis=0)
    mods = _ada_mods(cvec, ada_w, ada_b)
    xs = jnp.concatenate([x[0], ctx[0]], axis=0)
    rows = seq + n_ctx
    rope_tabs = _rope_tables(seq, n_ctx)
    ln_g2 = ln_g.reshape(depth * 2, 1, d)
    ln_b2 = ln_b.reshape(depth * 2, 1, d)
    conv_b3 = ffn_conv_b.reshape(depth, 1, -1)
    gqa_q_norm3 = gqa_q_norm.reshape(-1, 1, HEAD_DIM)
    gqa_k_norm3 = gqa_k_norm.reshape(-1, 1, HEAD_DIM)
    diff_subln3 = diff_subln.reshape(-1, 1, 2 * HEAD_DIM)
    fnet_b3 = fnet_b.reshape(-1, 1, d)

    (h,) = _ln_mod(xs, rows, seq, None, None, 0, mods, 0, 0, do_ln=False, emit_h=True)
    counts = [0, 0, 0]
    for i in range(depth):
        last = i == depth - 1
        kind = i % N_MIXERS
        j = counts[kind]
        counts[kind] += 1
        if kind == 0:
            n_q = d
            n_kv = d // GQA_GROUP
            q_t = _head_proj(h, gqa_w_qkv, j, 0, n_q, transpose_out=True, rope_tabs=rope_tabs,
                             gain=gqa_q_norm3, out_scale=q_scale)
            k = _head_proj(h, gqa_w_qkv, j, n_q, n_kv, transpose_out=False, rope_tabs=rope_tabs, gain=gqa_k_norm3)
            v_t = _head_proj(h, gqa_w_qkv, j, n_q + n_kv, n_kv, transpose_out=True)
            mixed, w_mix, bias = _gqa_attention(q_t, k, v_t, seq, n_ctx), gqa_w_o, None
        elif kind == 1:
            lambda_init = 0.8 - 0.6 * math.exp(-0.3 * i)
            q_t = _head_proj(h, diff_w_qkv, j, 0, d, transpose_out=True, rope_tabs=rope_tabs, out_scale=q_scale)
            k = _head_proj(h, diff_w_qkv, j, d, d, transpose_out=False, rope_tabs=rope_tabs)
            v_t = _head_proj(h, diff_w_qkv, j, 2 * d, d, transpose_out=True)
            mixed = _diff_attention(q_t, k, v_t, diff_lambda, diff_subln3, j, seq, n_ctx, lambda_init)
            w_mix, bias = diff_w_o, None
        else:
            mixed, w_mix, bias = _fourier_mix(h, seq, n_ctx), fnet_w, fnet_b3
        xs, h = _resid_ln(mixed, _weight_bf16(w_mix, j), bias, j, xs, rows, seq, mods, i, 2, ln_g2, ln_b2, 2 * i,
                          i, 3, alpha, emit_h=True)
        a = _ffn_up(h, ffn_w_up, ffn_conv_w, conv_b3, i, seq)
        w_down = _weight_bf16(ffn_w_down, i)
        if last:
            (xs,) = _resid_ln(a, w_down, None, 0, xs, seq, seq, mods, i, 5, ln_g2, ln_b2, 2 * i + 1,
                              0, 0, alpha, emit_h=False)
        else:
            xs, h = _resid_ln(a, w_down, None, 0, xs, rows, seq, mods, i, 5, ln_g2, ln_b2, 2 * i + 1,
                              i + 1, 0, alpha, emit_h=True)
    return xs[None]
```

```python
import functools
import math

import numpy as np
import jax
import jax.numpy as jnp
from jax import lax
from jax.experimental import pallas as pl
from jax.experimental.pallas import tpu as pltpu

HEAD_DIM = 128
GQA_GROUP = 4
GRID_W = 64
N_MIXERS = 3
FNET_GROUPS = 4
ROPE_THETA = 10000.0
EPS = 1e-6
ROW_GROUP = 256
FFT_MINOR = 128
BF16_SUBLANES = 16
LN_ROW_CHUNK = 32
V7X_VMEM_LIMIT_BYTES = 56 * 1024 * 1024

_F32 = jnp.float32
_BF16 = jnp.bfloat16


def _params(n_grid_axes):
    return pltpu.CompilerParams(dimension_semantics=("arbitrary",) * n_grid_axes,
                                vmem_limit_bytes=V7X_VMEM_LIMIT_BYTES)


def _largest_tile(n, unit, cap):
    best = None
    for t in range(unit, min(n, cap) + 1, unit):
        if n % t == 0:
            best = t
    assert best is not None, (n, unit, cap)
    return best


def _silu(x):
    return x * (1.0 / (1.0 + jnp.exp(-x)))


def _ada_kernel(cv_ref, w_ref, b_ref, o_ref):
    s = _silu(cv_ref[...]).astype(_BF16)
    acc = jnp.dot(s, w_ref[...].astype(_BF16), preferred_element_type=_F32)
    o_ref[...] = acc + b_ref[...]


def _ada_mods(cvec, ada_w, ada_b):
    depth, d, n = ada_w.shape
    tn = _largest_tile(n, 128, 1024)
    return pl.pallas_call(
        _ada_kernel,
        grid=(depth, n // tn),
        in_specs=[pl.BlockSpec((8, d), lambda l, j: (0, 0)),
                  pl.BlockSpec((None, d, tn), lambda l, j: (l, 0, j)),
                  pl.BlockSpec((None, 1, tn), lambda l, j: (l, 0, j))],
        out_specs=pl.BlockSpec((None, 8, tn), lambda l, j: (l, 0, j)),
        out_shape=jax.ShapeDtypeStruct((depth, 8, n), _F32),
        compiler_params=_params(2),
        name="ada_mods",
    )(cvec, ada_w, ada_b.reshape(depth, 1, n))


def _ln_mod_kernel(*refs, do_ln, emit_h, n_sub, n_lat_groups):
    it = iter(refs)
    z_ref = next(it)
    g_ref = next(it) if do_ln else None
    b_ref = next(it) if do_ln else None
    sh_ref = next(it) if emit_h else None
    sc_ref = next(it) if emit_h else None
    x_ref = next(it) if do_ln else None
    h_ref = next(it) if emit_h else None
    i = pl.program_id(0)
    for s in range(n_sub):
        rows = slice(s * ROW_GROUP, (s + 1) * ROW_GROUP)
        x = z_ref[rows, :]
        if do_ln:
            mu = jnp.mean(x, axis=-1, keepdims=True)
            xc = x - mu
            var = jnp.mean(xc * xc, axis=-1, keepdims=True)
            x = xc * lax.rsqrt(var + EPS) * g_ref[...] + b_ref[...]
            x_ref[rows, :] = x
        if emit_h:
            is_ctx = (i * n_sub + s) >= n_lat_groups
            sh = jnp.where(is_ctx, sh_ref[1:2, :], sh_ref[0:1, :])
            sc = jnp.where(is_ctx, sc_ref[1:2, :], sc_ref[0:1, :])
            h_ref[rows, :] = (x * (1.0 + sc) + sh).astype(_BF16)


def _ln_mod(z, n_rows, seq, ln_g2, ln_b2, ln_idx, mods, mod_layer, shift_idx, *, do_ln, emit_h):
    d = z.shape[1]
    tm = _largest_tile(n_rows, ROW_GROUP, 1024)
    in_specs = [pl.BlockSpec((tm, d), lambda i: (i, 0))]
    args = [z]
    if do_ln:
        in_specs += [pl.BlockSpec((None, 1, d), lambda i: (ln_idx, 0, 0))] * 2
        args += [ln_g2, ln_b2]
    if emit_h:
        in_specs += [pl.BlockSpec((None, 8, d), lambda i: (mod_layer, 0, shift_idx)),
                     pl.BlockSpec((None, 8, d), lambda i: (mod_layer, 0, shift_idx + 1))]
        args += [mods, mods]
    out_specs, out_shape = [], []
    if do_ln:
        out_specs.append(pl.BlockSpec((tm, d), lambda i: (i, 0)))
        out_shape.append(jax.ShapeDtypeStruct((n_rows, d), _F32))
    if emit_h:
        out_specs.append(pl.BlockSpec((tm, d), lambda i: (i, 0)))
        out_shape.append(jax.ShapeDtypeStruct((n_rows, d), _BF16))
    outs = pl.pallas_call(
        functools.partial(_ln_mod_kernel, do_ln=do_ln, emit_h=emit_h, n_sub=tm // ROW_GROUP,
                          n_lat_groups=seq // ROW_GROUP),
        grid=(n_rows // tm,),
        in_specs=in_specs, out_specs=out_specs, out_shape=out_shape,
        compiler_params=_params(1),
        name="ln_mod",
    )(*args)
    return outs


def _rope(y, cos, sin_signed):
    lane = lax.broadcasted_iota(jnp.int32, y.shape, 1)
    partner = jnp.where((lane % 64) < 32, pltpu.roll(y, 96, 1), pltpu.roll(y, 32, 1))
    return y * cos + partner * sin_signed


def _lagged_steps(i, n_tiles, produce, consume):
    @pl.when(jnp.logical_and(i < n_tiles, i % 2 == 0))
    def _():
        consume(1)
        produce(0)

    @pl.when(jnp.logical_and(i < n_tiles, i % 2 == 1))
    def _():
        consume(0)
        produce(1)

    @pl.when(i == n_tiles)
    def _():
        consume((n_tiles - 1) % 2)


def _lagged_index_maps(n_tiles):
    return (lambda i: jnp.minimum(i, n_tiles - 1)), (lambda i: jnp.maximum(i - 1, 0))


def _head_proj_kernel(*refs, n_tiles, rope, transpose_out, has_norm, out_scale):
    it = iter(refs)
    h_ref, w_ref = next(it), next(it)
    gain_ref = next(it) if has_norm else None
    cos_ref = next(it) if rope else None
    sin_ref = next(it) if rope else None
    o_ref, wbf_ref, raw_ref = next(it), next(it), next(it)
    i = pl.program_id(1)

    @pl.when(i == 0)
    def _():
        wbf_ref[...] = w_ref[...].astype(_BF16)
        raw_ref[1] = jnp.zeros(raw_ref.shape[1:], _F32)

    def produce(slot):
        raw_ref[slot] = jnp.dot(h_ref[...], wbf_ref[...], preferred_element_type=_F32)

    def consume(slot):
        for hh in range(raw_ref.shape[2] // HEAD_DIM):
            cols = slice(hh * HEAD_DIM, (hh + 1) * HEAD_DIM)
            y = raw_ref[slot, :, cols]
            if has_norm:
                ms = jnp.mean(y * y, axis=-1, keepdims=True)
                y = y * lax.rsqrt(ms + EPS) * gain_ref[...]
            if rope:
                y = _rope(y, cos_ref[...], sin_ref[...])
            if out_scale != 1.0:
                y = y * out_scale
            if transpose_out:
                o_ref[cols, :] = y.T.astype(_BF16)
            else:
                o_ref[:, cols] = y.astype(_BF16)

    _lagged_steps(i, n_tiles, produce, consume)


def _head_proj(h, w, w_layer, col0, n_cols, *, transpose_out, rope_tabs=None, gain=None, out_scale=1.0):
    rows, d = h.shape
    tm = _largest_tile(rows, ROW_GROUP, 1024)
    tn = _largest_tile(math.gcd(n_cols, col0) if col0 else n_cols, HEAD_DIM, 1024)
    j0 = col0 // tn
    n_tiles = rows // tm
    produced, consumed = _lagged_index_maps(n_tiles)
    in_specs = [pl.BlockSpec((tm, d), lambda j, i: (produced(i), 0)),
                pl.BlockSpec((None, d, tn), lambda j, i: (w_layer, 0, j0 + j))]
    args = [h, w]
    if gain is not None:
        in_specs.append(pl.BlockSpec((None, 1, HEAD_DIM), lambda j, i: (w_layer, 0, 0)))
        args.append(gain)
    if rope_tabs is not None:
        in_specs += [pl.BlockSpec((tm, HEAD_DIM), lambda j, i: (consumed(i), 0))] * 2
        args += list(rope_tabs)
    if transpose_out:
        out_spec = pl.BlockSpec((tn, tm), lambda j, i: (j, consumed(i)))
        out_shape = jax.ShapeDtypeStruct((n_cols, rows), _BF16)
    else:
        out_spec = pl.BlockSpec((tm, tn), lambda j, i: (consumed(i), j))
        out_shape = jax.ShapeDtypeStruct((rows, n_cols), _BF16)
    return pl.pallas_call(
        functools.partial(_head_proj_kernel, n_tiles=n_tiles, rope=rope_tabs is not None,
                          transpose_out=transpose_out, has_norm=gain is not None, out_scale=out_scale),
        grid=(n_cols // tn, n_tiles + 1),
        in_specs=in_specs, out_specs=out_spec, out_shape=out_shape,
        scratch_shapes=[pltpu.VMEM((d, tn), _BF16), pltpu.VMEM((2, tm, tn), _F32)],
        compiler_params=_params(2),
        name="head_proj",
    )(*args)


def _cast_kernel(w_ref, o_ref):
    o_ref[...] = w_ref[...].astype(_BF16)


def _weight_bf16(w, w_layer):
    _, k, n = w.shape
    tk = _largest_tile(k, BF16_SUBLANES, 512)
    return pl.pallas_call(
        _cast_kernel,
        grid=(k // tk,),
        in_specs=[pl.BlockSpec((None, tk, n), lambda r: (w_layer, r, 0))],
        out_specs=pl.BlockSpec((tk, n), lambda r: (r, 0)),
        out_shape=jax.ShapeDtypeStruct((k, n), _BF16),
        compiler_params=_params(1),
        name="weight_bf16",
    )(w)


def _resid_ln_kernel(*refs, n_tiles, alpha, has_bias, emit_h, n_lat_groups):
    it = iter(refs)
    a_ref, w_ref = next(it), next(it)
    b_ref = next(it) if has_bias else None
    x_ref, gate_ref, g_ref, be_ref = next(it), next(it), next(it), next(it)
    sh_ref = next(it) if emit_h else None
    sc_ref = next(it) if emit_h else None
    xo_ref = next(it)
    h_ref = next(it) if emit_h else None
    raw_ref = next(it)
    i = pl.program_id(0)
    tm = raw_ref.shape[1]

    @pl.when(i == 0)
    def _():
        raw_ref[1] = jnp.zeros(raw_ref.shape[1:], _F32)

    def produce(slot):
        raw_ref[slot] = jnp.dot(a_ref[...], w_ref[...], preferred_element_type=_F32)

    def consume(slot):
        is_ctx = (i - 1) >= n_lat_groups
        gate = jnp.where(is_ctx, gate_ref[1:2, :], gate_ref[0:1, :])
        if emit_h:
            sh = jnp.where(is_ctx, sh_ref[1:2, :], sh_ref[0:1, :])
            sc1 = 1.0 + jnp.where(is_ctx, sc_ref[1:2, :], sc_ref[0:1, :])
        for r0 in range(0, tm, LN_ROW_CHUNK):
            rows = slice(r0, r0 + LN_ROW_CHUNK)
            acc = raw_ref[slot, rows, :]
            if has_bias:
                acc = acc + b_ref[...]
            z = alpha * x_ref[rows, :] + gate * acc
            mu = jnp.mean(z, axis=-1, keepdims=True)
            zc = z - mu
            var = jnp.mean(zc * zc, axis=-1, keepdims=True)
            x = zc * lax.rsqrt(var + EPS) * g_ref[...] + be_ref[...]
            xo_ref[rows, :] = x
            if emit_h:
                h_ref[rows, :] = (x * sc1 + sh).astype(_BF16)

    _lagged_steps(i, n_tiles, produce, consume)


def _resid_ln(a, w_bf, bias, bias_layer, xs, n_rows, seq, mods, mod_layer, gate_idx, ln_g2, ln_b2, ln_idx,
              next_layer, next_shift_idx, alpha, *, emit_h):
    k, d = w_bf.shape
    tm = ROW_GROUP
    n_tiles = n_rows // tm
    produced, consumed = _lagged_index_maps(n_tiles)
    in_specs = [pl.BlockSpec((tm, k), lambda i: (produced(i), 0)),
                pl.BlockSpec((k, d), lambda i: (0, 0), pipeline_mode=pl.Buffered(1))]
    args = [a, w_bf]
    if bias is not None:
        in_specs.append(pl.BlockSpec((None, 1, d), lambda i: (bias_layer, 0, 0)))
        args.append(bias)
    in_specs += [pl.BlockSpec((tm, d), lambda i: (consumed(i), 0)),
                 pl.BlockSpec((None, 8, d), lambda i: (mod_layer, 0, gate_idx)),
                 pl.BlockSpec((None, 1, d), lambda i: (ln_idx, 0, 0)),
                 pl.BlockSpec((None, 1, d), lambda i: (ln_idx, 0, 0))]
    args += [xs, mods, ln_g2, ln_b2]
    out_specs = [pl.BlockSpec((tm, d), lambda i: (consumed(i), 0))]
    out_shape = [jax.ShapeDtypeStruct((n_rows, d), _F32)]
    if emit_h:
        in_specs += [pl.BlockSpec((None, 8, d), lambda i: (next_layer, 0, next_shift_idx)),
                     pl.BlockSpec((None, 8, d), lambda i: (next_layer, 0, next_shift_idx + 1))]
        args += [mods, mods]
        out_specs.append(pl.BlockSpec((tm, d), lambda i: (consumed(i), 0)))
        out_shape.append(jax.ShapeDtypeStruct((n_rows, d), _BF16))
    return pl.pallas_call(
        functools.partial(_resid_ln_kernel, n_tiles=n_tiles, alpha=alpha, has_bias=bias is not None,
                          emit_h=emit_h, n_lat_groups=seq // ROW_GROUP),
        grid=(n_tiles + 1,),
        in_specs=in_specs, out_specs=out_specs, out_shape=out_shape,
        scratch_shapes=[pltpu.VMEM((2, tm, d), _F32)],
        compiler_params=_params(1),
        name="resid_ln",
    )(*args)


def _ffn_up_kernel(hp_ref, hm_ref, hn_ref, wv_ref, wg_ref, cwv_ref, cwg_ref, cbv_ref, cbg_ref, o_ref,
                   wv_bf, wg_bf, hs_ref, *, seq, rows):
    i = pl.program_id(1)
    tm = hm_ref.shape[0]
    halo = BF16_SUBLANES

    @pl.when(i == 0)
    def _():
        wv_bf[...] = wv_ref[...].astype(_BF16)
        wg_bf[...] = wg_ref[...].astype(_BF16)

    hs_ref[0:halo, :] = hp_ref[...]
    hs_ref[halo:halo + tm, :] = hm_ref[...]
    hs_ref[halo + tm:, :] = hn_ref[...]
    hs = hs_ref[...]
    r = i * tm + lax.broadcasted_iota(jnp.int32, (tm, 1), 0)
    keep_prev = jnp.where(r == 0, 0.0, 1.0) * jnp.where(r == seq, 0.0, 1.0)
    keep_next = jnp.where(r == seq - 1, 0.0, 1.0) * jnp.where(r == rows - 1, 0.0, 1.0)

    def conv_branch(w_bf, cw_ref, cb_ref):
        u = jnp.dot(hs, w_bf[...], preferred_element_type=_F32)
        cw = cw_ref[...]
        return (cw[1:2, :] * u[halo:halo + tm, :]
                + keep_prev * (cw[0:1, :] * u[halo - 1:halo - 1 + tm, :])
                + keep_next * (cw[2:3, :] * u[halo + 1:halo + 1 + tm, :])
                + cb_ref[...])

    val = conv_branch(wv_bf, cwv_ref, cbv_ref)
    gate = conv_branch(wg_bf, cwg_ref, cbg_ref)
    o_ref[...] = (_silu(gate) * val).astype(_BF16)


def _ffn_up(h, w_up, conv_w, conv_b3, layer, seq):
    rows, d = h.shape
    d_ff = w_up.shape[2] // 2
    tm = _largest_tile(rows, ROW_GROUP, 1024)
    tf = _largest_tile(d_ff, 128, 512)
    nf = d_ff // tf
    halo = BF16_SUBLANES
    per_tile = tm // halo
    last_halo_blk = rows // halo - 1
    return pl.pallas_call(
        functools.partial(_ffn_up_kernel, seq=seq, rows=rows),
        grid=(nf, rows // tm),
        in_specs=[pl.BlockSpec((halo, d), lambda j, i: (jnp.maximum(i * per_tile - 1, 0), 0)),
                  pl.BlockSpec((tm, d), lambda j, i: (i, 0)),
                  pl.BlockSpec((halo, d), lambda j, i: (jnp.minimum((i + 1) * per_tile, last_halo_blk), 0)),
                  pl.BlockSpec((None, d, tf), lambda j, i: (layer, 0, j)),
                  pl.BlockSpec((None, d, tf), lambda j, i: (layer, 0, nf + j)),
                  pl.BlockSpec((None, 3, tf), lambda j, i: (layer, 0, j)),
                  pl.BlockSpec((None, 3, tf), lambda j, i: (layer, 0, nf + j)),
                  pl.BlockSpec((None, 1, tf), lambda j, i: (layer, 0, j)),
                  pl.BlockSpec((None, 1, tf), lambda j, i: (layer, 0, nf + j))],
        out_specs=pl.BlockSpec((tm, tf), lambda j, i: (i, j)),
        out_shape=jax.ShapeDtypeStruct((rows, d_ff), _BF16),
        scratch_shapes=[pltpu.VMEM((d, tf), _BF16), pltpu.VMEM((d, tf), _BF16),
                        pltpu.VMEM((tm + 2 * halo, d), _BF16)],
        compiler_params=_params(2),
        name="ffn_up",
    )(h, h, h, w_up, w_up, conv_w, conv_w, conv_b3, conv_b3)


_NEG_BIG = -1e30
_LOG2E = math.log2(math.e)
ATTN_KEY_CHUNK_CAP = 1024


def _key_chunk(seq, ctx):
    assert ctx % 128 == 0, ctx
    return _largest_tile(seq, 128, ATTN_KEY_CHUNK_CAP)


def _attend_t(qt_list, k_slices, vt_slice, acc_refs, s_ref, sc_ref, *, latent, seq, ctx, tk):
    n_maps = len(qt_list)
    tq = qt_list[0].shape[1]

    def scores(rows, c):
        return jnp.dot(k_slices[c](rows), qt_list[c], preferred_element_type=_F32)

    def absorb(s, vt, c, m, l):
        m_new = jnp.maximum(m, jnp.max(s, axis=0, keepdims=True))
        p = jnp.exp2(s - m_new)
        alpha = jnp.exp2(m - m_new)
        acc_refs[c][...] = alpha * acc_refs[c][...] + jnp.dot(vt, p.astype(_BF16), preferred_element_type=_F32)
        return m_new, alpha * l + jnp.sum(p, axis=0, keepdims=True)

    for acc_ref in acc_refs:
        acc_ref[...] = jnp.zeros(acc_ref.shape, _F32)
    carry = tuple((jnp.full((1, tq), _NEG_BIG, _F32), jnp.zeros((1, tq), _F32)) for _ in range(n_maps))
    ctx_rows = slice(seq, seq + ctx)
    n_chunks = seq // tk if latent else 0
    for c in range(n_maps):
        if latent:
            s_ref[0, c] = scores(slice(0, tk), c)
        else:
            sc_ref[c] = scores(ctx_rows, c)
    for n in range(n_chunks):
        for c in range(n_maps):
            if n + 1 < n_chunks:
                s_ref[(n + 1) % 2, c] = scores(slice((n + 1) * tk, (n + 2) * tk), c)
            else:
                sc_ref[c] = scores(ctx_rows, c)
        vt = vt_slice(slice(n * tk, (n + 1) * tk))
        carry = tuple(absorb(s_ref[n % 2, c], vt, c, *carry[c]) for c in range(n_maps))
    vt_ctx = vt_slice(ctx_rows)
    carry = tuple(absorb(sc_ref[c], vt_ctx, c, *carry[c]) for c in range(n_maps))
    return [l for (_, l) in carry]


def _per_query_block(i, n_latent_blocks, run):
    @pl.when(i < n_latent_blocks)
    def _():
        run(True)

    @pl.when(i >= n_latent_blocks)
    def _():
        run(False)


def _gqa_attn_kernel(qt_ref, k_ref, vt_ref, o_ref, s_ref, sc_ref, *acc_refs, seq, ctx, tk):
    tq = qt_ref.shape[1]

    def run(latent):
        qt_list = [qt_ref[g * HEAD_DIM:(g + 1) * HEAD_DIM, :] for g in range(GQA_GROUP)]
        k_slices = [lambda rows: k_ref[rows, :]] * GQA_GROUP
        sums = _attend_t(qt_list, k_slices, lambda rows: vt_ref[:, rows], acc_refs, s_ref, sc_ref,
                         latent=latent, seq=seq, ctx=ctx, tk=tk)
        for g in range(GQA_GROUP):
            o_t = acc_refs[g][...] * (1.0 / sums[g])
            o_ref[:, g * HEAD_DIM:(g + 1) * HEAD_DIM] = o_t.T.astype(_BF16)

    _per_query_block(pl.program_id(1), seq // tq, run)


def _gqa_attention(q_t, k, v_t, seq, ctx):
    dq, rows = q_t.shape
    kvh = k.shape[1] // HEAD_DIM
    tq = ROW_GROUP
    qw = GQA_GROUP * HEAD_DIM
    tk = _key_chunk(seq, ctx)
    return pl.pallas_call(
        functools.partial(_gqa_attn_kernel, seq=seq, ctx=ctx, tk=tk),
        grid=(kvh, rows // tq),
        in_specs=[pl.BlockSpec((qw, tq), lambda g, i: (g, i)),
                  pl.BlockSpec((rows, HEAD_DIM), lambda g, i: (0, g)),
                  pl.BlockSpec((HEAD_DIM, rows), lambda g, i: (g, 0))],
        out_specs=pl.BlockSpec((tq, qw), lambda g, i: (i, g)),
        out_shape=jax.ShapeDtypeStruct((rows, dq), _BF16),
        scratch_shapes=[pltpu.VMEM((2, GQA_GROUP, tk, tq), _F32), pltpu.VMEM((GQA_GROUP, ctx, tq), _F32)]
        + [pltpu.VMEM((HEAD_DIM, tq), _F32)] * GQA_GROUP,
        compiler_params=_params(2),
        name="gqa_attention",
    )(q_t, k, v_t)


def _diff_attn_kernel(qt_ref, k_ref, vt_ref, lam_ref, sub_ref, o_ref, s_ref, sc_ref, *acc_refs,
                      seq, ctx, tk, lambda_init):
    tq = qt_ref.shape[1]

    def run(latent):
        qt_list = [qt_ref[c * HEAD_DIM:(c + 1) * HEAD_DIM, :] for c in range(2)]
        k_slices = [functools.partial(lambda rows, c: k_ref[rows, c * HEAD_DIM:(c + 1) * HEAD_DIM], c=c)
                    for c in range(2)]
        l0, l1 = _attend_t(qt_list, k_slices, lambda rows: vt_ref[:, rows], acc_refs, s_ref, sc_ref,
                           latent=latent, seq=seq, ctx=ctx, tk=tk)
        lp = lam_ref[...]
        lam = (jnp.exp(jnp.sum(lp[0:1, :] * lp[1:2, :], axis=1, keepdims=True))
               - jnp.exp(jnp.sum(lp[2:3, :] * lp[3:4, :], axis=1, keepdims=True)) + lambda_init)
        o_t = acc_refs[0][...] * (1.0 / l0) - lam * (acc_refs[1][...] * (1.0 / l1))
        o = o_t.T
        ms = jnp.mean(o * o, axis=-1, keepdims=True)
        o = o * lax.rsqrt(ms + EPS) * sub_ref[...] * (1.0 - lambda_init)
        o_ref[...] = o.astype(_BF16)

    _per_query_block(pl.program_id(1), seq // tq, run)


def _diff_attention(q_t, k, v_t, lam_params, subln, layer_j, seq, ctx, lambda_init):
    d, rows = q_t.shape
    hw = 2 * HEAD_DIM
    tq = ROW_GROUP
    tk = _key_chunk(seq, ctx)
    return pl.pallas_call(
        functools.partial(_diff_attn_kernel, seq=seq, ctx=ctx, tk=tk, lambda_init=lambda_init),
        grid=(d // hw, rows // tq),
        in_specs=[pl.BlockSpec((hw, tq), lambda h, i: (h, i)),
                  pl.BlockSpec((rows, hw), lambda h, i: (0, h)),
                  pl.BlockSpec((hw, rows), lambda h, i: (h, 0)),
                  pl.BlockSpec((None, 4, HEAD_DIM), lambda h, i: (layer_j, 0, 0)),
                  pl.BlockSpec((None, 1, hw), lambda h, i: (layer_j, 0, 0))],
        out_specs=pl.BlockSpec((tq, hw), lambda h, i: (i, h)),
        out_shape=jax.ShapeDtypeStruct((rows, d), _BF16),
        scratch_shapes=[pltpu.VMEM((2, 2, tk, tq), _F32), pltpu.VMEM((2, ctx, tq), _F32)]
        + [pltpu.VMEM((hw, tq), _F32)] * 2,
        compiler_params=_params(2),
        name="diff_attention",
    )(q_t, k, v_t, lam_params, subln)


def _dft_cos_sin(n):
    kt = np.outer(np.arange(n), np.arange(n)) % n
    ang = 2.0 * np.pi * kt / n
    return np.cos(ang), np.sin(ang)


def _channel_dft(z, cs_ref, scale, store):
    p = z.shape[0] // 2
    gd = cs_ref.shape[1]
    zb = z.astype(_BF16)
    for g in range(z.shape[1] // gd):
        cols = slice(g * gd, (g + 1) * gd)
        zc = jnp.concatenate([zb[:p, cols], zb[p:, cols]], axis=1)
        y = jnp.dot(zc, cs_ref[...], preferred_element_type=_F32) * scale
        store(cols, y.astype(_BF16))


def _fft_stage1_kernel(a_ref, f1_ref, tw_ref, o_ref, *, d):
    n1 = a_ref.shape[0]
    for tt in range(a_ref.shape[1] // d):
        cols = slice(tt * d, (tt + 1) * d)
        a = jnp.dot(f1_ref[...], a_ref[:, cols], preferred_element_type=_F32)
        ar, ai = a[:n1, :], a[n1:, :]
        twr, twi = tw_ref[0, :, tt:tt + 1], tw_ref[1, :, tt:tt + 1]
        o_ref[0, :, cols] = (ar * twr - ai * twi).astype(_BF16)
        o_ref[1, :, cols] = (ar * twi + ai * twr).astype(_BF16)


def _fft_stage2_kernel(b_ref, g2_ref, cs_ref, y_ref, *, scale):
    d = b_ref.shape[3]
    for kk in range(b_ref.shape[1]):
        bcat = jnp.concatenate([b_ref[0, kk], b_ref[1, kk]], axis=0)
        z = jnp.dot(g2_ref[...], bcat, preferred_element_type=_F32)

        def store(cols, y, kk=kk):
            y_ref[:, kk * d + cols.start:kk * d + cols.stop] = y

        _channel_dft(z, cs_ref, scale, store)


def _ctx_dft_kernel(a_ref, fc_ref, cs_ref, y_ref, *, scale):
    z = jnp.dot(fc_ref[...], a_ref[...], preferred_element_type=_F32)

    def store(cols, y):
        y_ref[:, cols] = y

    _channel_dft(z, cs_ref, scale, store)


def _fourier_mix(h, seq, ctx):
    rows, d = h.shape
    gd = d // FNET_GROUPS
    n2 = FFT_MINOR
    n1 = seq // n2
    g2 = 16
    kb = 2
    assert seq % n2 == 0 and rows % n2 == 0 and n2 % g2 == 0 and n1 % kb == 0

    c1, s1 = _dft_cos_sin(n1)
    f1 = jnp.asarray(np.concatenate([c1, -s1], axis=0), _BF16)
    ang = 2.0 * np.pi * np.outer(np.arange(n1), np.arange(n2)) / seq
    tw = np.stack([np.cos(ang), -np.sin(ang)])
    tw = jnp.asarray(tw.reshape(2, n1, n2 // g2, g2).transpose(2, 0, 1, 3), _F32)
    c2, s2 = _dft_cos_sin(n2)
    g2m = jnp.asarray(np.block([[c2, s2], [-s2, c2]]), _BF16)
    cc, sc = _dft_cos_sin(gd)
    cs = jnp.asarray(np.concatenate([cc, sc], axis=0), _BF16)
    cx, sx = _dft_cos_sin(ctx)
    fc = jnp.asarray(np.concatenate([cx, -sx], axis=0), _BF16)

    b = pl.pallas_call(
        functools.partial(_fft_stage1_kernel, d=d),
        grid=(n2 // g2,),
        in_specs=[pl.BlockSpec((n1, g2 * d), lambda c: (0, c)),
                  pl.BlockSpec((2 * n1, n1), lambda c: (0, 0)),
                  pl.BlockSpec((None, 2, n1, g2), lambda c: (c, 0, 0, 0))],
        out_specs=pl.BlockSpec((2, n1, g2 * d), lambda c: (0, 0, c)),
        out_shape=jax.ShapeDtypeStruct((2, n1, n2 * d), _BF16),
        compiler_params=_params(1),
        name="fft_stage1",
    )(h.reshape(rows // n2, n2 * d), f1, tw)

    y_lat = pl.pallas_call(
        functools.partial(_fft_stage2_kernel, scale=1.0 / math.sqrt(seq * gd)),
        grid=(n1 // kb,),
        in_specs=[pl.BlockSpec((2, kb, n2, d), lambda k: (0, k, 0, 0)),
                  pl.BlockSpec((2 * n2, 2 * n2), lambda k: (0, 0)),
                  pl.BlockSpec((2 * gd, gd), lambda k: (0, 0))],
        out_specs=pl.BlockSpec((n2, kb * d), lambda k: (0, k)),
        out_shape=jax.ShapeDtypeStruct((n2, n1 * d), _BF16),
        compiler_params=_params(1),
        name="fft_stage2",
    )(b.reshape(2, n1, n2, d), g2m, cs)

    y_ctx = pl.pallas_call(
        functools.partial(_ctx_dft_kernel, scale=1.0 / math.sqrt(ctx * gd)),
        grid=(1,),
        in_specs=[pl.BlockSpec((ctx, d), lambda c: (seq // ctx, 0)),
                  pl.BlockSpec((2 * ctx, ctx), lambda c: (0, 0)),
                  pl.BlockSpec((2 * gd, gd), lambda c: (0, 0))],
        out_specs=pl.BlockSpec((ctx, d), lambda c: (0, 0)),
        out_shape=jax.ShapeDtypeStruct((ctx, d), _BF16),
        compiler_params=_params(1),
        name="ctx_dft",
    )(h, fc, cs)
    return jnp.concatenate([y_lat.reshape(seq, d), y_ctx], axis=0)


def _rope_tables(seq, ctx):
    quarter = HEAD_DIM // 4
    t = np.arange(seq)
    inv_freq = ROPE_THETA ** (-np.arange(quarter, dtype=np.float64) / quarter)
    ang_r = (t // GRID_W)[:, None] * inv_freq[None, :]
    ang_c = (t % GRID_W)[:, None] * inv_freq[None, :]
    cos = np.concatenate([np.cos(ang_r)] * 2 + [np.cos(ang_c)] * 2, axis=1)
    sin = np.concatenate([-np.sin(ang_r), np.sin(ang_r), -np.sin(ang_c), np.sin(ang_c)], axis=1)
    cos = np.concatenate([cos, np.ones((ctx, HEAD_DIM))], axis=0)
    sin = np.concatenate([sin, np.zeros((ctx, HEAD_DIM))], axis=0)
    return jnp.asarray(cos, _F32), jnp.asarray(sin, _F32)


def kernel(x, c, ctx, c_ctx, ada_w, ada_b, ln_g, ln_b, gqa_w_qkv, gqa_q_norm, gqa_k_norm, gqa_w_o,
           diff_w_qkv, diff_lambda, diff_subln, diff_w_o, fnet_w, fnet_b,
           ffn_w_up, ffn_conv_w, ffn_conv_b, ffn_w_down):
    batch, seq, d = x.shape
    n_ctx = ctx.shape[1]
    depth = ada_w.shape[0]
    assert batch == 1 and seq % ROW_GROUP == 0 and n_ctx % ROW_GROUP == 0 and seq % GRID_W == 0
    alpha = (2 * depth) ** 0.25
    q_scale = HEAD_DIM ** -0.5 * _LOG2E

    cvec = jnp.concatenate([c, c_ctx[None, :], jnp.zeros((6, d), _F32)], axis=0)
    mods = _ada_mods(cvec, ada_w, ada_b)
    xs = jnp.concatenate([x[0], ctx[0]], axis=0)
    rows = seq + n_ctx
    rope_tabs = _rope_tables(seq, n_ctx)
    ln_g2 = ln_g.reshape(depth * 2, 1, d)
    ln_b2 = ln_b.reshape(depth * 2, 1, d)
    conv_b3 = ffn_conv_b.reshape(depth, 1, -1)
    gqa_q_norm3 = gqa_q_norm.reshape(-1, 1, HEAD_DIM)
    gqa_k_norm3 = gqa_k_norm.reshape(-1, 1, HEAD_DIM)
    diff_subln3 = diff_subln.reshape(-1, 1, 2 * HEAD_DIM)
    fnet_b3 = fnet_b.reshape(-1, 1, d)

    (h,) = _ln_mod(xs, rows, seq, None, None, 0, mods, 0, 0, do_ln=False, emit_h=True)
    counts = [0, 0, 0]
    for i in range(depth):
        last = i == depth - 1
        kind = i % N_MIXERS
        j = counts[kind]
        counts[kind] += 1
        if kind == 0:
            n_q = d
            n_kv = d // GQA_GROUP
            q_t = _head_proj(h, gqa_w_qkv, j, 0, n_q, transpose_out=True, rope_tabs=rope_tabs,
                             gain=gqa_q_norm3, out_scale=q_scale)
            k = _head_proj(h, gqa_w_qkv, j, n_q, n_kv, transpose_out=False, rope_tabs=rope_tabs, gain=gqa_k_norm3)
            v_t = _head_proj(h, gqa_w_qkv, j, n_q + n_kv, n_kv, transpose_out=True)
            mixed, w_mix, bias = _gqa_attention(q_t, k, v_t, seq, n_ctx), gqa_w_o, None
        elif kind == 1:
            lambda_init = 0.8 - 0.6 * math.exp(-0.3 * i)
            q_t = _head_proj(h, diff_w_qkv, j, 0, d, transpose_out=True, rope_tabs=rope_tabs, out_scale=q_scale)
            k = _head_proj(h, diff_w_qkv, j, d, d, transpose_out=False, rope_tabs=rope_tabs)
            v_t = _head_proj(h, diff_w_qkv, j, 2 * d, d, transpose_out=True)
            mixed = _diff_attention(q_t, k, v_t, diff_lambda, diff_subln3, j, seq, n_ctx, lambda_init)
            w_mix, bias = diff_w_o, None
        else:
            mixed, w_mix, bias = _fourier_mix(h, seq, n_ctx), fnet_w, fnet_b3
        xs, h = _resid_ln(mixed, _weight_bf16(w_mix, j), bias, j, xs, rows, seq, mods, i, 2, ln_g2, ln_b2, 2 * i,
                          i, 3, alpha, emit_h=True)
        a = _ffn_up(h, ffn_w_up, ffn_conv_w, conv_b3, i, seq)
        w_down = _weight_bf16(ffn_w_down, i)
        if last:
            (xs,) = _resid_ln(a, w_down, None, 0, xs, seq, seq, mods, i, 5, ln_g2, ln_b2, 2 * i + 1,
                              0, 0, alpha, emit_h=False)
        else:
            xs, h = _resid_ln(a, w_down, None, 0, xs, rows, seq, mods, i, 5, ln_g2, ln_b2, 2 * i + 1,
                              i + 1, 0, alpha, emit_h=True)
    return xs[None]
```

```python
import functools
import math

import numpy as np
import jax
import jax.numpy as jnp
from jax import lax
from jax.experimental import pallas as pl
from jax.experimental.pallas import tpu as pltpu

HEAD_DIM = 128
GQA_GROUP = 4
GRID_W = 64
N_MIXERS = 3
FNET_GROUPS = 4
ROPE_THETA = 10000.0
EPS = 1e-6
ROW_GROUP = 256
FFT_MINOR = 128
FFT_STAGE1_POSITIONS = 16
FFT_STAGE2_FREQS = 2
BF16_SUBLANES = 16
LN_ROW_CHUNK = 32
LANES = 128
ROW_TILE_CAP = 1024
COL_TILE_CAP = 1024
FFN_COL_TILE_CAP = 512
CAST_ROW_TILE_CAP = 512
V7X_VMEM_LIMIT_BYTES = 56 * 1024 * 1024

_F32 = jnp.float32
_BF16 = jnp.bfloat16


def _params(n_grid_axes):
    return pltpu.CompilerParams(dimension_semantics=("arbitrary",) * n_grid_axes,
                                vmem_limit_bytes=V7X_VMEM_LIMIT_BYTES)


def _largest_tile(n, unit, cap):
    best = None
    for t in range(unit, min(n, cap) + 1, unit):
        if n % t == 0:
            best = t
    assert best is not None, (n, unit, cap)
    return best


def _silu(x):
    return x * (1.0 / (1.0 + jnp.exp(-x)))


def _ada_kernel(cv_ref, w_ref, b_ref, o_ref):
    s = _silu(cv_ref[...]).astype(_BF16)
    acc = jnp.dot(s, w_ref[...].astype(_BF16), preferred_element_type=_F32)
    o_ref[...] = acc + b_ref[...]


def _ada_mods(cvec, ada_w, ada_b):
    depth, d, n = ada_w.shape
    tn = _largest_tile(n, LANES, COL_TILE_CAP)
    return pl.pallas_call(
        _ada_kernel,
        grid=(depth, n // tn),
        in_specs=[pl.BlockSpec((8, d), lambda l, j: (0, 0)),
                  pl.BlockSpec((None, d, tn), lambda l, j: (l, 0, j)),
                  pl.BlockSpec((None, 1, tn), lambda l, j: (l, 0, j))],
        out_specs=pl.BlockSpec((None, 8, tn), lambda l, j: (l, 0, j)),
        out_shape=jax.ShapeDtypeStruct((depth, 8, n), _F32),
        compiler_params=_params(2),
        name="ada_mods",
    )(cvec, ada_w, ada_b.reshape(depth, 1, n))


def _modulate_kernel(x_ref, sh_ref, sc_ref, h_ref, *, n_sub, n_lat_groups):
    i = pl.program_id(0)
    for s in range(n_sub):
        rows = slice(s * ROW_GROUP, (s + 1) * ROW_GROUP)
        is_ctx = (i * n_sub + s) >= n_lat_groups
        sh = jnp.where(is_ctx, sh_ref[1:2, :], sh_ref[0:1, :])
        sc = jnp.where(is_ctx, sc_ref[1:2, :], sc_ref[0:1, :])
        h_ref[rows, :] = (x_ref[rows, :] * (1.0 + sc) + sh).astype(_BF16)


def _modulate(xs, seq, mods):
    rows, d = xs.shape
    tm = _largest_tile(rows, ROW_GROUP, ROW_TILE_CAP)
    return pl.pallas_call(
        functools.partial(_modulate_kernel, n_sub=tm // ROW_GROUP, n_lat_groups=seq // ROW_GROUP),
        grid=(rows // tm,),
        in_specs=[pl.BlockSpec((tm, d), lambda i: (i, 0)),
                  pl.BlockSpec((None, 8, d), lambda i: (0, 0, 0)),
                  pl.BlockSpec((None, 8, d), lambda i: (0, 0, 1))],
        out_specs=pl.BlockSpec((tm, d), lambda i: (i, 0)),
        out_shape=jax.ShapeDtypeStruct((rows, d), _BF16),
        compiler_params=_params(1),
        name="modulate",
    )(xs, mods, mods)


def _rope(y, cos, sin_signed):
    lane = lax.broadcasted_iota(jnp.int32, y.shape, 1)
    partner = jnp.where((lane % 64) < 32, pltpu.roll(y, 96, 1), pltpu.roll(y, 32, 1))
    return y * cos + partner * sin_signed


def _lagged_steps(i, n_tiles, produce, consume):
    @pl.when(jnp.logical_and(i < n_tiles, i % 2 == 0))
    def _():
        consume(1)
        produce(0)

    @pl.when(jnp.logical_and(i < n_tiles, i % 2 == 1))
    def _():
        consume(0)
        produce(1)

    @pl.when(i == n_tiles)
    def _():
        consume((n_tiles - 1) % 2)


def _lagged_index_maps(n_tiles):
    return (lambda i: jnp.minimum(i, n_tiles - 1)), (lambda i: jnp.maximum(i - 1, 0))


def _head_proj_kernel(*refs, n_tiles, rope, transpose_out, has_norm, out_scale):
    it = iter(refs)
    h_ref, w_ref = next(it), next(it)
    gain_ref = next(it) if has_norm else None
    cos_ref = next(it) if rope else None
    sin_ref = next(it) if rope else None
    o_ref, wbf_ref, raw_ref = next(it), next(it), next(it)
    i = pl.program_id(1)

    @pl.when(i == 0)
    def _():
        wbf_ref[...] = w_ref[...].astype(_BF16)
        raw_ref[1] = jnp.zeros(raw_ref.shape[1:], _F32)

    def produce(slot):
        raw_ref[slot] = jnp.dot(h_ref[...], wbf_ref[...], preferred_element_type=_F32)

    def consume(slot):
        for hh in range(raw_ref.shape[2] // HEAD_DIM):
            cols = slice(hh * HEAD_DIM, (hh + 1) * HEAD_DIM)
            y = raw_ref[slot, :, cols]
            if has_norm:
                ms = jnp.mean(y * y, axis=-1, keepdims=True)
                y = y * lax.rsqrt(ms + EPS) * gain_ref[...]
            if rope:
                y = _rope(y, cos_ref[...], sin_ref[...])
            if out_scale != 1.0:
                y = y * out_scale
            if transpose_out:
                o_ref[cols, :] = y.T.astype(_BF16)
            else:
                o_ref[:, cols] = y.astype(_BF16)

    _lagged_steps(i, n_tiles, produce, consume)


def _head_proj(h, w, w_layer, col0, n_cols, *, transpose_out, rope_tabs=None, gain=None, out_scale=1.0):
    rows, d = h.shape
    tm = _largest_tile(rows, ROW_GROUP, ROW_TILE_CAP)
    tn = _largest_tile(math.gcd(n_cols, col0) if col0 else n_cols, HEAD_DIM, COL_TILE_CAP)
    j0 = col0 // tn
    n_tiles = rows // tm
    produced, consumed = _lagged_index_maps(n_tiles)
    in_specs = [pl.BlockSpec((tm, d), lambda j, i: (produced(i), 0)),
                pl.BlockSpec((None, d, tn), lambda j, i: (w_layer, 0, j0 + j))]
    args = [h, w]
    if gain is not None:
        in_specs.append(pl.BlockSpec((None, 1, HEAD_DIM), lambda j, i: (w_layer, 0, 0)))
        args.append(gain)
    if rope_tabs is not None:
        in_specs += [pl.BlockSpec((tm, HEAD_DIM), lambda j, i: (consumed(i), 0))] * 2
        args += list(rope_tabs)
    if transpose_out:
        out_spec = pl.BlockSpec((tn, tm), lambda j, i: (j, consumed(i)))
        out_shape = jax.ShapeDtypeStruct((n_cols, rows), _BF16)
    else:
        out_spec = pl.BlockSpec((tm, tn), lambda j, i: (consumed(i), j))
        out_shape = jax.ShapeDtypeStruct((rows, n_cols), _BF16)
    return pl.pallas_call(
        functools.partial(_head_proj_kernel, n_tiles=n_tiles, rope=rope_tabs is not None,
                          transpose_out=transpose_out, has_norm=gain is not None, out_scale=out_scale),
        grid=(n_cols // tn, n_tiles + 1),
        in_specs=in_specs, out_specs=out_spec, out_shape=out_shape,
        scratch_shapes=[pltpu.VMEM((d, tn), _BF16), pltpu.VMEM((2, tm, tn), _F32)],
        compiler_params=_params(2),
        name="head_proj",
    )(*args)


def _cast_kernel(w_ref, o_ref):
    o_ref[...] = w_ref[...].astype(_BF16)


def _weight_bf16(w, w_layer):
    _, k, n = w.shape
    tk = _largest_tile(k, BF16_SUBLANES, CAST_ROW_TILE_CAP)
    return pl.pallas_call(
        _cast_kernel,
        grid=(k // tk,),
        in_specs=[pl.BlockSpec((None, tk, n), lambda r: (w_layer, r, 0))],
        out_specs=pl.BlockSpec((tk, n), lambda r: (r, 0)),
        out_shape=jax.ShapeDtypeStruct((k, n), _BF16),
        compiler_params=_params(1),
        name="weight_bf16",
    )(w)


def _resid_ln_kernel(*refs, n_tiles, alpha, has_bias, emit_h, n_lat_groups):
    it = iter(refs)
    a_ref, w_ref = next(it), next(it)
    b_ref = next(it) if has_bias else None
    x_ref, gate_ref, g_ref, be_ref = next(it), next(it), next(it), next(it)
    sh_ref = next(it) if emit_h else None
    sc_ref = next(it) if emit_h else None
    xo_ref = next(it)
    h_ref = next(it) if emit_h else None
    raw_ref = next(it)
    i = pl.program_id(0)
    tm = raw_ref.shape[1]

    @pl.when(i == 0)
    def _():
        raw_ref[1] = jnp.zeros(raw_ref.shape[1:], _F32)

    def produce(slot):
        raw_ref[slot] = jnp.dot(a_ref[...], w_ref[...], preferred_element_type=_F32)

    def consume(slot):
        is_ctx = (i - 1) >= n_lat_groups
        gate = jnp.where(is_ctx, gate_ref[1:2, :], gate_ref[0:1, :])
        if emit_h:
            sh = jnp.where(is_ctx, sh_ref[1:2, :], sh_ref[0:1, :])
            sc1 = 1.0 + jnp.where(is_ctx, sc_ref[1:2, :], sc_ref[0:1, :])
        for r0 in range(0, tm, LN_ROW_CHUNK):
            rows = slice(r0, r0 + LN_ROW_CHUNK)
            acc = raw_ref[slot, rows, :]
            if has_bias:
                acc = acc + b_ref[...]
            z = alpha * x_ref[rows, :] + gate * acc
            mu = jnp.mean(z, axis=-1, keepdims=True)
            zc = z - mu
            var = jnp.mean(zc * zc, axis=-1, keepdims=True)
            x = zc * lax.rsqrt(var + EPS) * g_ref[...] + be_ref[...]
            xo_ref[rows, :] = x
            if emit_h:
                h_ref[rows, :] = (x * sc1 + sh).astype(_BF16)

    _lagged_steps(i, n_tiles, produce, consume)


def _resid_ln(a, w_bf, bias, bias_layer, xs, n_rows, seq, mods, mod_layer, gate_idx, ln_g2, ln_b2, ln_idx,
              next_layer, next_shift_idx, alpha, *, emit_h):
    k, d = w_bf.shape
    tm = ROW_GROUP
    n_tiles = n_rows // tm
    produced, consumed = _lagged_index_maps(n_tiles)
    in_specs = [pl.BlockSpec((tm, k), lambda i: (produced(i), 0)),
                pl.BlockSpec((k, d), lambda i: (0, 0), pipeline_mode=pl.Buffered(1))]
    args = [a, w_bf]
    if bias is not None:
        in_specs.append(pl.BlockSpec((None, 1, d), lambda i: (bias_layer, 0, 0)))
        args.append(bias)
    in_specs += [pl.BlockSpec((tm, d), lambda i: (consumed(i), 0)),
                 pl.BlockSpec((None, 8, d), lambda i: (mod_layer, 0, gate_idx)),
                 pl.BlockSpec((None, 1, d), lambda i: (ln_idx, 0, 0)),
                 pl.BlockSpec((None, 1, d), lambda i: (ln_idx, 0, 0))]
    args += [xs, mods, ln_g2, ln_b2]
    out_specs = [pl.BlockSpec((tm, d), lambda i: (consumed(i), 0))]
    out_shape = [jax.ShapeDtypeStruct((n_rows, d), _F32)]
    if emit_h:
        in_specs += [pl.BlockSpec((None, 8, d), lambda i: (next_layer, 0, next_shift_idx)),
                     pl.BlockSpec((None, 8, d), lambda i: (next_layer, 0, next_shift_idx + 1))]
        args += [mods, mods]
        out_specs.append(pl.BlockSpec((tm, d), lambda i: (consumed(i), 0)))
        out_shape.append(jax.ShapeDtypeStruct((n_rows, d), _BF16))
    return pl.pallas_call(
        functools.partial(_resid_ln_kernel, n_tiles=n_tiles, alpha=alpha, has_bias=bias is not None,
                          emit_h=emit_h, n_lat_groups=seq // ROW_GROUP),
        grid=(n_tiles + 1,),
        in_specs=in_specs, out_specs=out_specs, out_shape=out_shape,
        scratch_shapes=[pltpu.VMEM((2, tm, d), _F32)],
        compiler_params=_params(1),
        name="resid_ln",
    )(*args)


def _ffn_up_kernel(hp_ref, hm_ref, hn_ref, wv_ref, wg_ref, cwv_ref, cwg_ref, cbv_ref, cbg_ref, o_ref,
                   wv_bf, wg_bf, hs_ref, *, seq, rows):
    i = pl.program_id(1)
    tm = hm_ref.shape[0]
    halo = BF16_SUBLANES

    @pl.when(i == 0)
    def _():
        wv_bf[...] = wv_ref[...].astype(_BF16)
        wg_bf[...] = wg_ref[...].astype(_BF16)

    hs_ref[0:halo, :] = hp_ref[...]
    hs_ref[halo:halo + tm, :] = hm_ref[...]
    hs_ref[halo + tm:, :] = hn_ref[...]
    hs = hs_ref[...]
    r = i * tm + lax.broadcasted_iota(jnp.int32, (tm, 1), 0)
    keep_prev = jnp.where(r == 0, 0.0, 1.0) * jnp.where(r == seq, 0.0, 1.0)
    keep_next = jnp.where(r == seq - 1, 0.0, 1.0) * jnp.where(r == rows - 1, 0.0, 1.0)

    def conv_branch(w_bf, cw_ref, cb_ref):
        u = jnp.dot(hs, w_bf[...], preferred_element_type=_F32)
        cw = cw_ref[...]
        return (cw[1:2, :] * u[halo:halo + tm, :]
                + keep_prev * (cw[0:1, :] * u[halo - 1:halo - 1 + tm, :])
                + keep_next * (cw[2:3, :] * u[halo + 1:halo + 1 + tm, :])
                + cb_ref[...])

    val = conv_branch(wv_bf, cwv_ref, cbv_ref)
    gate = conv_branch(wg_bf, cwg_ref, cbg_ref)
    o_ref[...] = (_silu(gate) * val).astype(_BF16)


def _ffn_up(h, w_up, conv_w, conv_b3, layer, seq):
    rows, d = h.shape
    d_ff = w_up.shape[2] // 2
    tm = _largest_tile(rows, ROW_GROUP, ROW_TILE_CAP)
    tf = _largest_tile(d_ff, LANES, FFN_COL_TILE_CAP)
    nf = d_ff // tf
    halo = BF16_SUBLANES
    per_tile = tm // halo
    last_halo_blk = rows // halo - 1
    return pl.pallas_call(
        functools.partial(_ffn_up_kernel, seq=seq, rows=rows),
        grid=(nf, rows // tm),
        in_specs=[pl.BlockSpec((halo, d), lambda j, i: (jnp.maximum(i * per_tile - 1, 0), 0)),
                  pl.BlockSpec((tm, d), lambda j, i: (i, 0)),
                  pl.BlockSpec((halo, d), lambda j, i: (jnp.minimum((i + 1) * per_tile, last_halo_blk), 0)),
                  pl.BlockSpec((None, d, tf), lambda j, i: (layer, 0, j)),
                  pl.BlockSpec((None, d, tf), lambda j, i: (layer, 0, nf + j)),
                  pl.BlockSpec((None, 3, tf), lambda j, i: (layer, 0, j)),
                  pl.BlockSpec((None, 3, tf), lambda j, i: (layer, 0, nf + j)),
                  pl.BlockSpec((None, 1, tf), lambda j, i: (layer, 0, j)),
                  pl.BlockSpec((None, 1, tf), lambda j, i: (layer, 0, nf + j))],
        out_specs=pl.BlockSpec((tm, tf), lambda j, i: (i, j)),
        out_shape=jax.ShapeDtypeStruct((rows, d_ff), _BF16),
        scratch_shapes=[pltpu.VMEM((d, tf), _BF16), pltpu.VMEM((d, tf), _BF16),
                        pltpu.VMEM((tm + 2 * halo, d), _BF16)],
        compiler_params=_params(2),
        name="ffn_up",
    )(h, h, h, w_up, w_up, conv_w, conv_w, conv_b3, conv_b3)


_NEG_BIG = -1e30
_LOG2E = math.log2(math.e)
ATTN_KEY_CHUNK_CAP = 1024


def _key_chunk(seq, ctx):
    assert ctx % 128 == 0, ctx
    return _largest_tile(seq, LANES, ATTN_KEY_CHUNK_CAP)


def _attend_t(qt_list, k_slices, vt_slice, acc_refs, s_ref, sc_ref, *, latent, seq, ctx, tk):
    n_maps = len(qt_list)
    tq = qt_list[0].shape[1]

    def scores(rows, c):
        return jnp.dot(k_slices[c](rows), qt_list[c], preferred_element_type=_F32)

    def absorb(s, vt, c, m, l):
        m_new = jnp.maximum(m, jnp.max(s, axis=0, keepdims=True))
        p = jnp.exp2(s - m_new)
        alpha = jnp.exp2(m - m_new)
        acc_refs[c][...] = alpha * acc_refs[c][...] + jnp.dot(vt, p.astype(_BF16), preferred_element_type=_F32)
        return m_new, alpha * l + jnp.sum(p, axis=0, keepdims=True)

    for acc_ref in acc_refs:
        acc_ref[...] = jnp.zeros(acc_ref.shape, _F32)
    carry = tuple((jnp.full((1, tq), _NEG_BIG, _F32), jnp.zeros((1, tq), _F32)) for _ in range(n_maps))
    ctx_rows = slice(seq, seq + ctx)
    n_chunks = seq // tk if latent else 0
    for c in range(n_maps):
        if latent:
            s_ref[0, c] = scores(slice(0, tk), c)
        else:
            sc_ref[c] = scores(ctx_rows, c)
    for n in range(n_chunks):
        for c in range(n_maps):
            if n + 1 < n_chunks:
                s_ref[(n + 1) % 2, c] = scores(slice((n + 1) * tk, (n + 2) * tk), c)
            else:
                sc_ref[c] = scores(ctx_rows, c)
        vt = vt_slice(slice(n * tk, (n + 1) * tk))
        carry = tuple(absorb(s_ref[n % 2, c], vt, c, *carry[c]) for c in range(n_maps))
    vt_ctx = vt_slice(ctx_rows)
    carry = tuple(absorb(sc_ref[c], vt_ctx, c, *carry[c]) for c in range(n_maps))
    return [l for (_, l) in carry]


def _per_query_block(i, n_latent_blocks, run):
    @pl.when(i < n_latent_blocks)
    def _():
        run(True)

    @pl.when(i >= n_latent_blocks)
    def _():
        run(False)


def _gqa_attn_kernel(qt_ref, k_ref, vt_ref, o_ref, s_ref, sc_ref, *acc_refs, seq, ctx, tk):
    tq = qt_ref.shape[1]

    def run(latent):
        qt_list = [qt_ref[g * HEAD_DIM:(g + 1) * HEAD_DIM, :] for g in range(GQA_GROUP)]
        k_slices = [lambda rows: k_ref[rows, :]] * GQA_GROUP
        sums = _attend_t(qt_list, k_slices, lambda rows: vt_ref[:, rows], acc_refs, s_ref, sc_ref,
                         latent=latent, seq=seq, ctx=ctx, tk=tk)
        for g in range(GQA_GROUP):
            o_t = acc_refs[g][...] * (1.0 / sums[g])
            o_ref[:, g * HEAD_DIM:(g + 1) * HEAD_DIM] = o_t.T.astype(_BF16)

    _per_query_block(pl.program_id(1), seq // tq, run)


def _gqa_attention(q_t, k, v_t, seq, ctx):
    dq, rows = q_t.shape
    kvh = k.shape[1] // HEAD_DIM
    tq = ROW_GROUP
    qw = GQA_GROUP * HEAD_DIM
    tk = _key_chunk(seq, ctx)
    return pl.pallas_call(
        functools.partial(_gqa_attn_kernel, seq=seq, ctx=ctx, tk=tk),
        grid=(kvh, rows // tq),
        in_specs=[pl.BlockSpec((qw, tq), lambda g, i: (g, i)),
                  pl.BlockSpec((rows, HEAD_DIM), lambda g, i: (0, g)),
                  pl.BlockSpec((HEAD_DIM, rows), lambda g, i: (g, 0))],
        out_specs=pl.BlockSpec((tq, qw), lambda g, i: (i, g)),
        out_shape=jax.ShapeDtypeStruct((rows, dq), _BF16),
        scratch_shapes=[pltpu.VMEM((2, GQA_GROUP, tk, tq), _F32), pltpu.VMEM((GQA_GROUP, ctx, tq), _F32)]
        + [pltpu.VMEM((HEAD_DIM, tq), _F32)] * GQA_GROUP,
        compiler_params=_params(2),
        name="gqa_attention",
    )(q_t, k, v_t)


def _diff_attn_kernel(qt_ref, k_ref, vt_ref, lam_ref, sub_ref, o_ref, s_ref, sc_ref, *acc_refs,
                      seq, ctx, tk, lambda_init):
    tq = qt_ref.shape[1]

    def run(latent):
        qt_list = [qt_ref[c * HEAD_DIM:(c + 1) * HEAD_DIM, :] for c in range(2)]
        k_slices = [functools.partial(lambda rows, c: k_ref[rows, c * HEAD_DIM:(c + 1) * HEAD_DIM], c=c)
                    for c in range(2)]
        l0, l1 = _attend_t(qt_list, k_slices, lambda rows: vt_ref[:, rows], acc_refs, s_ref, sc_ref,
                           latent=latent, seq=seq, ctx=ctx, tk=tk)
        lp = lam_ref[...]
        lam = (jnp.exp(jnp.sum(lp[0:1, :] * lp[1:2, :], axis=1, keepdims=True))
               - jnp.exp(jnp.sum(lp[2:3, :] * lp[3:4, :], axis=1, keepdims=True)) + lambda_init)
        o_t = acc_refs[0][...] * (1.0 / l0) - lam * (acc_refs[1][...] * (1.0 / l1))
        o = o_t.T
        ms = jnp.mean(o * o, axis=-1, keepdims=True)
        o = o * lax.rsqrt(ms + EPS) * sub_ref[...] * (1.0 - lambda_init)
        o_ref[...] = o.astype(_BF16)

    _per_query_block(pl.program_id(1), seq // tq, run)


def _diff_attention(q_t, k, v_t, lam_params, subln, layer_j, seq, ctx, lambda_init):
    d, rows = q_t.shape
    hw = 2 * HEAD_DIM
    tq = ROW_GROUP
    tk = _key_chunk(seq, ctx)
    return pl.pallas_call(
        functools.partial(_diff_attn_kernel, seq=seq, ctx=ctx, tk=tk, lambda_init=lambda_init),
        grid=(d // hw, rows // tq),
        in_specs=[pl.BlockSpec((hw, tq), lambda h, i: (h, i)),
                  pl.BlockSpec((rows, hw), lambda h, i: (0, h)),
                  pl.BlockSpec((hw, rows), lambda h, i: (h, 0)),
                  pl.BlockSpec((None, 4, HEAD_DIM), lambda h, i: (layer_j, 0, 0)),
                  pl.BlockSpec((None, 1, hw), lambda h, i: (layer_j, 0, 0))],
        out_specs=pl.BlockSpec((tq, hw), lambda h, i: (i, h)),
        out_shape=jax.ShapeDtypeStruct((rows, d), _BF16),
        scratch_shapes=[pltpu.VMEM((2, 2, tk, tq), _F32), pltpu.VMEM((2, ctx, tq), _F32)]
        + [pltpu.VMEM((hw, tq), _F32)] * 2,
        compiler_params=_params(2),
        name="diff_attention",
    )(q_t, k, v_t, lam_params, subln)


def _dft_cos_sin(n):
    kt = np.outer(np.arange(n), np.arange(n)) % n
    ang = 2.0 * np.pi * kt / n
    return np.cos(ang), np.sin(ang)


def _channel_dft(z, cs, scale, store):
    p = z.shape[0] // 2
    gd = cs.shape[1]
    zb = z.astype(_BF16)
    for g in range(z.shape[1] // gd):
        cols = slice(g * gd, (g + 1) * gd)
        zc = jnp.concatenate([zb[:p, cols], zb[p:, cols]], axis=1)
        y = jnp.dot(zc, cs, preferred_element_type=_F32) * scale
        store(cols, y.astype(_BF16))


def _fft_stage1_kernel(a_ref, f1_ref, tw_ref, o_ref, *, d):
    n1 = a_ref.shape[0]
    f1 = f1_ref[...].astype(_BF16)
    for tt in range(a_ref.shape[1] // d):
        cols = slice(tt * d, (tt + 1) * d)
        a = jnp.dot(f1, a_ref[:, cols], preferred_element_type=_F32)
        ar, ai = a[:n1, :], a[n1:, :]
        twr, twi = tw_ref[0, :, tt:tt + 1], tw_ref[1, :, tt:tt + 1]
        o_ref[0, :, cols] = (ar * twr - ai * twi).astype(_BF16)
        o_ref[1, :, cols] = (ar * twi + ai * twr).astype(_BF16)


def _fft_stage2_kernel(b_ref, g2_ref, cs_ref, y_ref, g2_bf, cs_bf, *, scale):
    @pl.when(pl.program_id(0) == 0)
    def _():
        g2_bf[...] = g2_ref[...].astype(_BF16)
        cs_bf[...] = cs_ref[...].astype(_BF16)

    d = b_ref.shape[3]
    for kk in range(b_ref.shape[1]):
        bcat = jnp.concatenate([b_ref[0, kk], b_ref[1, kk]], axis=0)
        z = jnp.dot(g2_bf[...], bcat, preferred_element_type=_F32)

        def store(cols, y, kk=kk):
            y_ref[:, kk * d + cols.start:kk * d + cols.stop] = y

        _channel_dft(z, cs_bf[...], scale, store)


def _ctx_dft_kernel(a_ref, fc_ref, cs_ref, y_ref, *, scale):
    z = jnp.dot(fc_ref[...].astype(_BF16), a_ref[...], preferred_element_type=_F32)

    def store(cols, y):
        y_ref[:, cols] = y

    _channel_dft(z, cs_ref[...].astype(_BF16), scale, store)


def _fourier_mix(h, seq, ctx):
    rows, d = h.shape
    gd = d // FNET_GROUPS
    n2 = FFT_MINOR
    n1 = seq // n2
    g2 = FFT_STAGE1_POSITIONS
    kb = FFT_STAGE2_FREQS
    assert seq % n2 == 0 and rows % n2 == 0 and n2 % g2 == 0 and n1 % kb == 0

    c1, s1 = _dft_cos_sin(n1)
    f1 = jnp.asarray(np.concatenate([c1, -s1], axis=0), _F32)
    ang = 2.0 * np.pi * np.outer(np.arange(n1), np.arange(n2)) / seq
    tw = np.stack([np.cos(ang), -np.sin(ang)])
    tw = jnp.asarray(tw.reshape(2, n1, n2 // g2, g2).transpose(2, 0, 1, 3), _F32)
    c2, s2 = _dft_cos_sin(n2)
    g2m = jnp.asarray(np.block([[c2, s2], [-s2, c2]]), _F32)
    cc, sc = _dft_cos_sin(gd)
    cs = jnp.asarray(np.concatenate([cc, sc], axis=0), _F32)
    cx, sx = _dft_cos_sin(ctx)
    fc = jnp.asarray(np.concatenate([cx, -sx], axis=0), _F32)

    b = pl.pallas_call(
        functools.partial(_fft_stage1_kernel, d=d),
        grid=(n2 // g2,),
        in_specs=[pl.BlockSpec((n1, g2 * d), lambda c: (0, c)),
                  pl.BlockSpec((2 * n1, n1), lambda c: (0, 0)),
                  pl.BlockSpec((None, 2, n1, g2), lambda c: (c, 0, 0, 0))],
        out_specs=pl.BlockSpec((2, n1, g2 * d), lambda c: (0, 0, c)),
        out_shape=jax.ShapeDtypeStruct((2, n1, n2 * d), _BF16),
        compiler_params=_params(1),
        name="fft_stage1",
    )(h.reshape(rows // n2, n2 * d), f1, tw)

    y_lat = pl.pallas_call(
        functools.partial(_fft_stage2_kernel, scale=1.0 / math.sqrt(seq * gd)),
        grid=(n1 // kb,),
        in_specs=[pl.BlockSpec((2, kb, n2, d), lambda k: (0, k, 0, 0)),
                  pl.BlockSpec((2 * n2, 2 * n2), lambda k: (0, 0)),
                  pl.BlockSpec((2 * gd, gd), lambda k: (0, 0))],
        out_specs=pl.BlockSpec((n2, kb * d), lambda k: (0, k)),
        out_shape=jax.ShapeDtypeStruct((n2, n1 * d), _BF16),
        scratch_shapes=[pltpu.VMEM((2 * n2, 2 * n2), _BF16), pltpu.VMEM((2 * gd, gd), _BF16)],
        compiler_params=_params(1),
        name="fft_stage2",
    )(b.reshape(2, n1, n2, d), g2m, cs)

    y_ctx = pl.pallas_call(
        functools.partial(_ctx_dft_kernel, scale=1.0 / math.sqrt(ctx * gd)),
        grid=(1,),
        in_specs=[pl.BlockSpec((ctx, d), lambda c: (seq // ctx, 0)),
                  pl.BlockSpec((2 * ctx, ctx), lambda c: (0, 0)),
                  pl.BlockSpec((2 * gd, gd), lambda c: (0, 0))],
        out_specs=pl.BlockSpec((ctx, d), lambda c: (0, 0)),
        out_shape=jax.ShapeDtypeStruct((ctx, d), _BF16),
        compiler_params=_params(1),
        name="ctx_dft",
    )(h, fc, cs)
    return jnp.concatenate([y_lat.reshape(seq, d), y_ctx], axis=0)


def _rope_tables(seq, ctx):
    quarter = HEAD_DIM // 4
    t = np.arange(seq)
    inv_freq = ROPE_THETA ** (-np.arange(quarter, dtype=np.float64) / quarter)
    ang_r = (t // GRID_W)[:, None] * inv_freq[None, :]
    ang_c = (t % GRID_W)[:, None] * inv_freq[None, :]
    cos = np.concatenate([np.cos(ang_r)] * 2 + [np.cos(ang_c)] * 2, axis=1)
    sin = np.concatenate([-np.sin(ang_r), np.sin(ang_r), -np.sin(ang_c), np.sin(ang_c)], axis=1)
    cos = np.concatenate([cos, np.ones((ctx, HEAD_DIM))], axis=0)
    sin = np.concatenate([sin, np.zeros((ctx, HEAD_DIM))], axis=0)
    return jnp.asarray(cos, _F32), jnp.asarray(sin, _F32)


def kernel(x, c, ctx, c_ctx, ada_w, ada_b, ln_g, ln_b, gqa_w_qkv, gqa_q_norm, gqa_k_norm, gqa_w_o,
           diff_w_qkv, diff_lambda, diff_subln, diff_w_o, fnet_w, fnet_b,
           ffn_w_up, ffn_conv_w, ffn_conv_b, ffn_w_down):
    batch, seq, d = x.shape
    n_ctx = ctx.shape[1]
    depth = ada_w.shape[0]
    assert batch == 1 and seq % ROW_GROUP == 0 and n_ctx % ROW_GROUP == 0 and seq % GRID_W == 0
    alpha = (2 * depth) ** 0.25
    q_scale = HEAD_DIM ** -0.5 * _LOG2E

    cvec = jnp.concatenate([c, c_ctx[None, :], jnp.zeros((6, d), _F32)], axis=0)
    mods = _ada_mods(cvec, ada_w, ada_b)
    xs = jnp.concatenate([x[0], ctx[0]], axis=0)
    rows = seq + n_ctx
    rope_tabs = _rope_tables(seq, n_ctx)
    ln_g2 = ln_g.reshape(depth * 2, 1, d)
    ln_b2 = ln_b.reshape(depth * 2, 1, d)
    conv_b3 = ffn_conv_b.reshape(depth, 1, -1)
    gqa_q_norm3 = gqa_q_norm.reshape(-1, 1, HEAD_DIM)
    gqa_k_norm3 = gqa_k_norm.reshape(-1, 1, HEAD_DIM)
    diff_subln3 = diff_subln.reshape(-1, 1, 2 * HEAD_DIM)
    fnet_b3 = fnet_b.reshape(-1, 1, d)

    h = _modulate(xs, seq, mods)
    counts = [0, 0, 0]
    for i in range(depth):
        last = i == depth - 1
        kind = i % N_MIXERS
        j = counts[kind]
        counts[kind] += 1
        if kind == 0:
            n_q = d
            n_kv = d // GQA_GROUP
            q_t = _head_proj(h, gqa_w_qkv, j, 0, n_q, transpose_out=True, rope_tabs=rope_tabs,
                             gain=gqa_q_norm3, out_scale=q_scale)
            k = _head_proj(h, gqa_w_qkv, j, n_q, n_kv, transpose_out=False, rope_tabs=rope_tabs, gain=gqa_k_norm3)
            v_t = _head_proj(h, gqa_w_qkv, j, n_q + n_kv, n_kv, transpose_out=True)
            mixed, w_mix, bias = _gqa_attention(q_t, k, v_t, seq, n_ctx), gqa_w_o, None
        elif kind == 1:
            lambda_init = 0.8 - 0.6 * math.exp(-0.3 * i)
            q_t = _head_proj(h, diff_w_qkv, j, 0, d, transpose_out=True, rope_tabs=rope_tabs, out_scale=q_scale)
            k = _head_proj(h, diff_w_qkv, j, d, d, transpose_out=False, rope_tabs=rope_tabs)
            v_t = _head_proj(h, diff_w_qkv, j, 2 * d, d, transpose_out=True)
            mixed = _diff_attention(q_t, k, v_t, diff_lambda, diff_subln3, j, seq, n_ctx, lambda_init)
            w_mix, bias = diff_w_o, None
        else:
            mixed, w_mix, bias = _fourier_mix(h, seq, n_ctx), fnet_w, fnet_b3
        xs, h = _resid_ln(mixed, _weight_bf16(w_mix, j), bias, j, xs, rows, seq, mods, i, 2, ln_g2, ln_b2, 2 * i,
                          i, 3, alpha, emit_h=True)
        a = _ffn_up(h, ffn_w_up, ffn_conv_w, conv_b3, i, seq)
        w_down = _weight_bf16(ffn_w_down, i)
        if last:
            (xs,) = _resid_ln(a, w_down, None, 0, xs, seq, seq, mods, i, 5, ln_g2, ln_b2, 2 * i + 1,
                              0, 0, alpha, emit_h=False)
        else:
            xs, h = _resid_ln(a, w_down, None, 0, xs, rows, seq, mods, i, 5, ln_g2, ln_b2, 2 * i + 1,
                              i + 1, 0, alpha, emit_h=True)
    return xs[None]
```

```python
import functools
import math

import numpy as np
import jax
import jax.numpy as jnp
from jax import lax
from jax.experimental import pallas as pl
from jax.experimental.pallas import tpu as pltpu

HEAD_DIM = 128
GQA_GROUP = 4
GRID_W = 64
N_MIXERS = 3
FNET_GROUPS = 4
ROPE_THETA = 10000.0
EPS = 1e-6
ROW_GROUP = 256
FFT_MINOR = 128
FFT_STAGE1_POSITIONS = 16
FFT_STAGE2_FREQS = 2
BF16_SUBLANES = 16
LN_ROW_CHUNK = 32
LANES = 128
ROW_TILE_CAP = 1024
COL_TILE_CAP = 1024
FFN_COL_TILE_CAP = 512
CAST_ROW_TILE_CAP = 512
V7X_VMEM_LIMIT_BYTES = 56 * 1024 * 1024

_F32 = jnp.float32
_BF16 = jnp.bfloat16


def _params(n_grid_axes):
    return pltpu.CompilerParams(dimension_semantics=("arbitrary",) * n_grid_axes,
                                vmem_limit_bytes=V7X_VMEM_LIMIT_BYTES)


def _largest_tile(n, unit, cap):
    best = None
    for t in range(unit, min(n, cap) + 1, unit):
        if n % t == 0:
            best = t
    assert best is not None, (n, unit, cap)
    return best


def _silu(x):
    return x * (1.0 / (1.0 + jnp.exp(-x)))


def _ada_kernel(cv_ref, w_ref, b_ref, o_ref):
    s = _silu(cv_ref[...]).astype(_BF16)
    acc = jnp.dot(s, w_ref[...].astype(_BF16), preferred_element_type=_F32)
    o_ref[...] = acc + b_ref[...]


def _ada_mods(cvec, ada_w, ada_b):
    depth, d, n = ada_w.shape
    tn = _largest_tile(n, LANES, COL_TILE_CAP)
    return pl.pallas_call(
        _ada_kernel,
        grid=(depth, n // tn),
        in_specs=[pl.BlockSpec((8, d), lambda l, j: (0, 0)),
                  pl.BlockSpec((None, d, tn), lambda l, j: (l, 0, j)),
                  pl.BlockSpec((None, 1, tn), lambda l, j: (l, 0, j))],
        out_specs=pl.BlockSpec((None, 8, tn), lambda l, j: (l, 0, j)),
        out_shape=jax.ShapeDtypeStruct((depth, 8, n), _F32),
        compiler_params=_params(2),
        name="ada_mods",
    )(cvec, ada_w, ada_b.reshape(depth, 1, n))


def _modulate_kernel(x_ref, sh_ref, sc_ref, h_ref, *, n_sub, n_lat_groups):
    i = pl.program_id(0)
    for s in range(n_sub):
        rows = slice(s * ROW_GROUP, (s + 1) * ROW_GROUP)
        is_ctx = (i * n_sub + s) >= n_lat_groups
        sh = jnp.where(is_ctx, sh_ref[1:2, :], sh_ref[0:1, :])
        sc = jnp.where(is_ctx, sc_ref[1:2, :], sc_ref[0:1, :])
        h_ref[rows, :] = (x_ref[rows, :] * (1.0 + sc) + sh).astype(_BF16)


def _modulate(xs, seq, mods):
    rows, d = xs.shape
    tm = _largest_tile(rows, ROW_GROUP, ROW_TILE_CAP)
    return pl.pallas_call(
        functools.partial(_modulate_kernel, n_sub=tm // ROW_GROUP, n_lat_groups=seq // ROW_GROUP),
        grid=(rows // tm,),
        in_specs=[pl.BlockSpec((tm, d), lambda i: (i, 0)),
                  pl.BlockSpec((None, 8, d), lambda i: (0, 0, 0)),
                  pl.BlockSpec((None, 8, d), lambda i: (0, 0, 1))],
        out_specs=pl.BlockSpec((tm, d), lambda i: (i, 0)),
        out_shape=jax.ShapeDtypeStruct((rows, d), _BF16),
        compiler_params=_params(1),
        name="modulate",
    )(xs, mods, mods)


def _rope(y, cos, sin_signed):
    lane = lax.broadcasted_iota(jnp.int32, y.shape, 1)
    partner = jnp.where((lane % 64) < 32, pltpu.roll(y, 96, 1), pltpu.roll(y, 32, 1))
    return y * cos + partner * sin_signed


def _lagged_steps(i, n_tiles, produce, consume):
    @pl.when(jnp.logical_and(i < n_tiles, i % 2 == 0))
    def _():
        consume(1)
        produce(0)

    @pl.when(jnp.logical_and(i < n_tiles, i % 2 == 1))
    def _():
        consume(0)
        produce(1)

    @pl.when(i == n_tiles)
    def _():
        consume((n_tiles - 1) % 2)


def _lagged_index_maps(n_tiles):
    return (lambda i: jnp.minimum(i, n_tiles - 1)), (lambda i: jnp.maximum(i - 1, 0))


def _head_proj_kernel(*refs, n_tiles, rope, transpose_out, has_norm, out_scale):
    it = iter(refs)
    h_ref, w_ref = next(it), next(it)
    gain_ref = next(it) if has_norm else None
    cos_ref = next(it) if rope else None
    sin_ref = next(it) if rope else None
    o_ref, wbf_ref, raw_ref = next(it), next(it), next(it)
    i = pl.program_id(1)

    @pl.when(i == 0)
    def _():
        wbf_ref[...] = w_ref[...].astype(_BF16)
        raw_ref[1] = jnp.zeros(raw_ref.shape[1:], _F32)

    def produce(slot):
        raw_ref[slot] = jnp.dot(h_ref[...], wbf_ref[...], preferred_element_type=_F32)

    def consume(slot):
        for hh in range(raw_ref.shape[2] // HEAD_DIM):
            cols = slice(hh * HEAD_DIM, (hh + 1) * HEAD_DIM)
            y = raw_ref[slot, :, cols]
            if has_norm:
                ms = jnp.mean(y * y, axis=-1, keepdims=True)
                y = y * lax.rsqrt(ms + EPS) * gain_ref[...]
            if rope:
                y = _rope(y, cos_ref[...], sin_ref[...])
            if out_scale != 1.0:
                y = y * out_scale
            if transpose_out:
                o_ref[cols, :] = y.T.astype(_BF16)
            else:
                o_ref[:, cols] = y.astype(_BF16)

    _lagged_steps(i, n_tiles, produce, consume)


def _head_proj(h, w, w_layer, col0, n_cols, *, transpose_out, rope_tabs=None, gain=None, out_scale=1.0):
    rows, d = h.shape
    tm = _largest_tile(rows, ROW_GROUP, ROW_TILE_CAP)
    tn = _largest_tile(math.gcd(n_cols, col0) if col0 else n_cols, HEAD_DIM, COL_TILE_CAP)
    j0 = col0 // tn
    n_tiles = rows // tm
    produced, consumed = _lagged_index_maps(n_tiles)
    in_specs = [pl.BlockSpec((tm, d), lambda j, i: (produced(i), 0)),
                pl.BlockSpec((None, d, tn), lambda j, i: (w_layer, 0, j0 + j))]
    args = [h, w]
    if gain is not None:
        in_specs.append(pl.BlockSpec((None, 1, HEAD_DIM), lambda j, i: (w_layer, 0, 0)))
        args.append(gain)
    if rope_tabs is not None:
        in_specs += [pl.BlockSpec((tm, HEAD_DIM), lambda j, i: (consumed(i), 0))] * 2
        args += list(rope_tabs)
    if transpose_out:
        out_spec = pl.BlockSpec((tn, tm), lambda j, i: (j, consumed(i)))
        out_shape = jax.ShapeDtypeStruct((n_cols, rows), _BF16)
    else:
        out_spec = pl.BlockSpec((tm, tn), lambda j, i: (consumed(i), j))
        out_shape = jax.ShapeDtypeStruct((rows, n_cols), _BF16)
    return pl.pallas_call(
        functools.partial(_head_proj_kernel, n_tiles=n_tiles, rope=rope_tabs is not None,
                          transpose_out=transpose_out, has_norm=gain is not None, out_scale=out_scale),
        grid=(n_cols // tn, n_tiles + 1),
        in_specs=in_specs, out_specs=out_spec, out_shape=out_shape,
        scratch_shapes=[pltpu.VMEM((d, tn), _BF16), pltpu.VMEM((2, tm, tn), _F32)],
        compiler_params=_params(2),
        name="head_proj",
    )(*args)


def _cast_kernel(w_ref, o_ref):
    o_ref[...] = w_ref[...].astype(_BF16)


def _weight_bf16(w, w_layer):
    _, k, n = w.shape
    tk = _largest_tile(k, BF16_SUBLANES, CAST_ROW_TILE_CAP)
    return pl.pallas_call(
        _cast_kernel,
        grid=(k // tk,),
        in_specs=[pl.BlockSpec((None, tk, n), lambda r: (w_layer, r, 0))],
        out_specs=pl.BlockSpec((tk, n), lambda r: (r, 0)),
        out_shape=jax.ShapeDtypeStruct((k, n), _BF16),
        compiler_params=_params(1),
        name="weight_bf16",
    )(w)


def _side_cast_fits(k_rows, n_rows):
    return k_rows // _largest_tile(k_rows, BF16_SUBLANES, CAST_ROW_TILE_CAP) <= n_rows // ROW_GROUP


def _resid_ln_kernel(*refs, n_tiles, alpha, has_bias, emit_h, n_cast, n_lat_groups):
    it = iter(refs)
    a_ref, w_ref = next(it), next(it)
    b_ref = next(it) if has_bias else None
    x_ref, gate_ref, g_ref, be_ref = next(it), next(it), next(it), next(it)
    sh_ref = next(it) if emit_h else None
    sc_ref = next(it) if emit_h else None
    cast_in_ref = next(it) if n_cast else None
    xo_ref = next(it)
    h_ref = next(it) if emit_h else None
    cast_out_ref = next(it) if n_cast else None
    raw_ref = next(it)
    i = pl.program_id(0)
    tm = raw_ref.shape[1]

    @pl.when(i == 0)
    def _():
        raw_ref[1] = jnp.zeros(raw_ref.shape[1:], _F32)

    if n_cast:
        @pl.when(i < n_cast)
        def _():
            cast_out_ref[...] = cast_in_ref[...].astype(_BF16)

    def produce(slot):
        raw_ref[slot] = jnp.dot(a_ref[...], w_ref[...], preferred_element_type=_F32)

    def consume(slot):
        is_ctx = (i - 1) >= n_lat_groups
        gate = jnp.where(is_ctx, gate_ref[1:2, :], gate_ref[0:1, :])
        if emit_h:
            sh = jnp.where(is_ctx, sh_ref[1:2, :], sh_ref[0:1, :])
            sc1 = 1.0 + jnp.where(is_ctx, sc_ref[1:2, :], sc_ref[0:1, :])
        for r0 in range(0, tm, LN_ROW_CHUNK):
            rows = slice(r0, r0 + LN_ROW_CHUNK)
            acc = raw_ref[slot, rows, :]
            if has_bias:
                acc = acc + b_ref[...]
            z = alpha * x_ref[rows, :] + gate * acc
            mu = jnp.mean(z, axis=-1, keepdims=True)
            zc = z - mu
            var = jnp.mean(zc * zc, axis=-1, keepdims=True)
            x = zc * lax.rsqrt(var + EPS) * g_ref[...] + be_ref[...]
            xo_ref[rows, :] = x
            if emit_h:
                h_ref[rows, :] = (x * sc1 + sh).astype(_BF16)

    _lagged_steps(i, n_tiles, produce, consume)


def _resid_ln(a, w_bf, bias, bias_layer, xs, n_rows, seq, mods, mod_layer, gate_idx, ln_g2, ln_b2, ln_idx,
              next_layer, next_shift_idx, alpha, *, emit_h, cast_src=None, cast_layer=0):
    k, d = w_bf.shape
    tm = ROW_GROUP
    n_tiles = n_rows // tm
    produced, consumed = _lagged_index_maps(n_tiles)
    n_cast = 0
    if cast_src is not None:
        _, kc, dc = cast_src.shape
        tkc = _largest_tile(kc, BF16_SUBLANES, CAST_ROW_TILE_CAP)
        n_cast = kc // tkc
        assert _side_cast_fits(kc, n_rows), (n_cast, n_tiles)
    in_specs = [pl.BlockSpec((tm, k), lambda i: (produced(i), 0)),
                pl.BlockSpec((k, d), lambda i: (0, 0), pipeline_mode=pl.Buffered(1))]
    args = [a, w_bf]
    if bias is not None:
        in_specs.append(pl.BlockSpec((None, 1, d), lambda i: (bias_layer, 0, 0)))
        args.append(bias)
    in_specs += [pl.BlockSpec((tm, d), lambda i: (consumed(i), 0)),
                 pl.BlockSpec((None, 8, d), lambda i: (mod_layer, 0, gate_idx)),
                 pl.BlockSpec((None, 1, d), lambda i: (ln_idx, 0, 0)),
                 pl.BlockSpec((None, 1, d), lambda i: (ln_idx, 0, 0))]
    args += [xs, mods, ln_g2, ln_b2]
    out_specs = [pl.BlockSpec((tm, d), lambda i: (consumed(i), 0))]
    out_shape = [jax.ShapeDtypeStruct((n_rows, d), _F32)]
    if emit_h:
        in_specs += [pl.BlockSpec((None, 8, d), lambda i: (next_layer, 0, next_shift_idx)),
                     pl.BlockSpec((None, 8, d), lambda i: (next_layer, 0, next_shift_idx + 1))]
        args += [mods, mods]
        out_specs.append(pl.BlockSpec((tm, d), lambda i: (consumed(i), 0)))
        out_shape.append(jax.ShapeDtypeStruct((n_rows, d), _BF16))
    if n_cast:
        in_specs.append(pl.BlockSpec((None, tkc, dc), lambda i: (cast_layer, jnp.minimum(i, n_cast - 1), 0)))
        args.append(cast_src)
        out_specs.append(pl.BlockSpec((tkc, dc), lambda i: (jnp.minimum(i, n_cast - 1), 0)))
        out_shape.append(jax.ShapeDtypeStruct((kc, dc), _BF16))
    return pl.pallas_call(
        functools.partial(_resid_ln_kernel, n_tiles=n_tiles, alpha=alpha, has_bias=bias is not None,
                          emit_h=emit_h, n_cast=n_cast, n_lat_groups=seq // ROW_GROUP),
        grid=(n_tiles + 1,),
        in_specs=in_specs, out_specs=out_specs, out_shape=out_shape,
        scratch_shapes=[pltpu.VMEM((2, tm, d), _F32)],
        compiler_params=_params(1),
        name="resid_ln",
    )(*args)


def _ffn_up_kernel(hp_ref, hm_ref, hn_ref, wv_ref, wg_ref, cwv_ref, cwg_ref, cbv_ref, cbg_ref, o_ref,
                   wv_bf, wg_bf, hs_ref, *, seq, rows):
    i = pl.program_id(1)
    tm = hm_ref.shape[0]
    halo = BF16_SUBLANES

    @pl.when(i == 0)
    def _():
        wv_bf[...] = wv_ref[...].astype(_BF16)
        wg_bf[...] = wg_ref[...].astype(_BF16)

    hs_ref[0:halo, :] = hp_ref[...]
    hs_ref[halo:halo + tm, :] = hm_ref[...]
    hs_ref[halo + tm:, :] = hn_ref[...]
    hs = hs_ref[...]
    r = i * tm + lax.broadcasted_iota(jnp.int32, (tm, 1), 0)
    keep_prev = jnp.where(r == 0, 0.0, 1.0) * jnp.where(r == seq, 0.0, 1.0)
    keep_next = jnp.where(r == seq - 1, 0.0, 1.0) * jnp.where(r == rows - 1, 0.0, 1.0)

    def conv_branch(w_bf, cw_ref, cb_ref):
        u = jnp.dot(hs, w_bf[...], preferred_element_type=_F32)
        cw = cw_ref[...]
        return (cw[1:2, :] * u[halo:halo + tm, :]
                + keep_prev * (cw[0:1, :] * u[halo - 1:halo - 1 + tm, :])
                + keep_next * (cw[2:3, :] * u[halo + 1:halo + 1 + tm, :])
                + cb_ref[...])

    val = conv_branch(wv_bf, cwv_ref, cbv_ref)
    gate = conv_branch(wg_bf, cwg_ref, cbg_ref)
    o_ref[...] = (_silu(gate) * val).astype(_BF16)


def _ffn_up(h, w_up, conv_w, conv_b3, layer, seq):
    rows, d = h.shape
    d_ff = w_up.shape[2] // 2
    tm = _largest_tile(rows, ROW_GROUP, ROW_TILE_CAP)
    tf = _largest_tile(d_ff, LANES, FFN_COL_TILE_CAP)
    nf = d_ff // tf
    halo = BF16_SUBLANES
    per_tile = tm // halo
    last_halo_blk = rows // halo - 1
    return pl.pallas_call(
        functools.partial(_ffn_up_kernel, seq=seq, rows=rows),
        grid=(nf, rows // tm),
        in_specs=[pl.BlockSpec((halo, d), lambda j, i: (jnp.maximum(i * per_tile - 1, 0), 0)),
                  pl.BlockSpec((tm, d), lambda j, i: (i, 0)),
                  pl.BlockSpec((halo, d), lambda j, i: (jnp.minimum((i + 1) * per_tile, last_halo_blk), 0)),
                  pl.BlockSpec((None, d, tf), lambda j, i: (layer, 0, j)),
                  pl.BlockSpec((None, d, tf), lambda j, i: (layer, 0, nf + j)),
                  pl.BlockSpec((None, 3, tf), lambda j, i: (layer, 0, j)),
                  pl.BlockSpec((None, 3, tf), lambda j, i: (layer, 0, nf + j)),
                  pl.BlockSpec((None, 1, tf), lambda j, i: (layer, 0, j)),
                  pl.BlockSpec((None, 1, tf), lambda j, i: (layer, 0, nf + j))],
        out_specs=pl.BlockSpec((tm, tf), lambda j, i: (i, j)),
        out_shape=jax.ShapeDtypeStruct((rows, d_ff), _BF16),
        scratch_shapes=[pltpu.VMEM((d, tf), _BF16), pltpu.VMEM((d, tf), _BF16),
                        pltpu.VMEM((tm + 2 * halo, d), _BF16)],
        compiler_params=_params(2),
        name="ffn_up",
    )(h, h, h, w_up, w_up, conv_w, conv_w, conv_b3, conv_b3)


_NEG_BIG = -1e30
_LOG2E = math.log2(math.e)
ATTN_KEY_CHUNK_CAP = 1024


def _key_chunk(seq, ctx):
    assert ctx % 128 == 0, ctx
    return _largest_tile(seq, LANES, ATTN_KEY_CHUNK_CAP)


def _attend_t(qt_list, k_slices, vt_slice, acc_refs, s_ref, sc_ref, *, latent, seq, ctx, tk):
    n_maps = len(qt_list)
    tq = qt_list[0].shape[1]

    def scores(rows, c):
        return jnp.dot(k_slices[c](rows), qt_list[c], preferred_element_type=_F32)

    def absorb(s, vt, c, m, l):
        m_new = jnp.maximum(m, jnp.max(s, axis=0, keepdims=True))
        p = jnp.exp2(s - m_new)
        alpha = jnp.exp2(m - m_new)
        acc_refs[c][...] = alpha * acc_refs[c][...] + jnp.dot(vt, p.astype(_BF16), preferred_element_type=_F32)
        return m_new, alpha * l + jnp.sum(p, axis=0, keepdims=True)

    for acc_ref in acc_refs:
        acc_ref[...] = jnp.zeros(acc_ref.shape, _F32)
    carry = tuple((jnp.full((1, tq), _NEG_BIG, _F32), jnp.zeros((1, tq), _F32)) for _ in range(n_maps))
    ctx_rows = slice(seq, seq + ctx)
    n_chunks = seq // tk if latent else 0
    for c in range(n_maps):
        if latent:
            s_ref[0, c] = scores(slice(0, tk), c)
        else:
            sc_ref[c] = scores(ctx_rows, c)
    for n in range(n_chunks):
        for c in range(n_maps):
            if n + 1 < n_chunks:
                s_ref[(n + 1) % 2, c] = scores(slice((n + 1) * tk, (n + 2) * tk), c)
            else:
                sc_ref[c] = scores(ctx_rows, c)
        vt = vt_slice(slice(n * tk, (n + 1) * tk))
        carry = tuple(absorb(s_ref[n % 2, c], vt, c, *carry[c]) for c in range(n_maps))
    vt_ctx = vt_slice(ctx_rows)
    carry = tuple(absorb(sc_ref[c], vt_ctx, c, *carry[c]) for c in range(n_maps))
    return [l for (_, l) in carry]


def _per_query_block(i, n_latent_blocks, run):
    @pl.when(i < n_latent_blocks)
    def _():
        run(True)

    @pl.when(i >= n_latent_blocks)
    def _():
        run(False)


def _gqa_attn_kernel(qt_ref, k_ref, vt_ref, o_ref, s_ref, sc_ref, *acc_refs, seq, ctx, tk):
    tq = qt_ref.shape[1]

    def run(latent):
        qt_list = [qt_ref[g * HEAD_DIM:(g + 1) * HEAD_DIM, :] for g in range(GQA_GROUP)]
        k_slices = [lambda rows: k_ref[rows, :]] * GQA_GROUP
        sums = _attend_t(qt_list, k_slices, lambda rows: vt_ref[:, rows], acc_refs, s_ref, sc_ref,
                         latent=latent, seq=seq, ctx=ctx, tk=tk)
        for g in range(GQA_GROUP):
            o_t = acc_refs[g][...] * (1.0 / sums[g])
            o_ref[:, g * HEAD_DIM:(g + 1) * HEAD_DIM] = o_t.T.astype(_BF16)

    _per_query_block(pl.program_id(1), seq // tq, run)


def _gqa_attention(q_t, k, v_t, seq, ctx):
    dq, rows = q_t.shape
    kvh = k.shape[1] // HEAD_DIM
    tq = ROW_GROUP
    qw = GQA_GROUP * HEAD_DIM
    tk = _key_chunk(seq, ctx)
    return pl.pallas_call(
        functools.partial(_gqa_attn_kernel, seq=seq, ctx=ctx, tk=tk),
        grid=(kvh, rows // tq),
        in_specs=[pl.BlockSpec((qw, tq), lambda g, i: (g, i)),
                  pl.BlockSpec((rows, HEAD_DIM), lambda g, i: (0, g)),
                  pl.BlockSpec((HEAD_DIM, rows), lambda g, i: (g, 0))],
        out_specs=pl.BlockSpec((tq, qw), lambda g, i: (i, g)),
        out_shape=jax.ShapeDtypeStruct((rows, dq), _BF16),
        scratch_shapes=[pltpu.VMEM((2, GQA_GROUP, tk, tq), _F32), pltpu.VMEM((GQA_GROUP, ctx, tq), _F32)]
        + [pltpu.VMEM((HEAD_DIM, tq), _F32)] * GQA_GROUP,
        compiler_params=_params(2),
        name="gqa_attention",
    )(q_t, k, v_t)


def _diff_attn_kernel(qt_ref, k_ref, vt_ref, lam_ref, sub_ref, o_ref, s_ref, sc_ref, *acc_refs,
                      seq, ctx, tk, lambda_init):
    tq = qt_ref.shape[1]

    def run(latent):
        qt_list = [qt_ref[c * HEAD_DIM:(c + 1) * HEAD_DIM, :] for c in range(2)]
        k_slices = [functools.partial(lambda rows, c: k_ref[rows, c * HEAD_DIM:(c + 1) * HEAD_DIM], c=c)
                    for c in range(2)]
        l0, l1 = _attend_t(qt_list, k_slices, lambda rows: vt_ref[:, rows], acc_refs, s_ref, sc_ref,
                           latent=latent, seq=seq, ctx=ctx, tk=tk)
        lp = lam_ref[...]
        lam = (jnp.exp(jnp.sum(lp[0:1, :] * lp[1:2, :], axis=1, keepdims=True))
               - jnp.exp(jnp.sum(lp[2:3, :] * lp[3:4, :], axis=1, keepdims=True)) + lambda_init)
        o_t = acc_refs[0][...] * (1.0 / l0) - lam * (acc_refs[1][...] * (1.0 / l1))
        o = o_t.T
        ms = jnp.mean(o * o, axis=-1, keepdims=True)
        o = o * lax.rsqrt(ms + EPS) * sub_ref[...] * (1.0 - lambda_init)
        o_ref[...] = o.astype(_BF16)

    _per_query_block(pl.program_id(1), seq // tq, run)


def _diff_attention(q_t, k, v_t, lam_params, subln, layer_j, seq, ctx, lambda_init):
    d, rows = q_t.shape
    hw = 2 * HEAD_DIM
    tq = ROW_GROUP
    tk = _key_chunk(seq, ctx)
    return pl.pallas_call(
        functools.partial(_diff_attn_kernel, seq=seq, ctx=ctx, tk=tk, lambda_init=lambda_init),
        grid=(d // hw, rows // tq),
        in_specs=[pl.BlockSpec((hw, tq), lambda h, i: (h, i)),
                  pl.BlockSpec((rows, hw), lambda h, i: (0, h)),
                  pl.BlockSpec((hw, rows), lambda h, i: (h, 0)),
                  pl.BlockSpec((None, 4, HEAD_DIM), lambda h, i: (layer_j, 0, 0)),
                  pl.BlockSpec((None, 1, hw), lambda h, i: (layer_j, 0, 0))],
        out_specs=pl.BlockSpec((tq, hw), lambda h, i: (i, h)),
        out_shape=jax.ShapeDtypeStruct((rows, d), _BF16),
        scratch_shapes=[pltpu.VMEM((2, 2, tk, tq), _F32), pltpu.VMEM((2, ctx, tq), _F32)]
        + [pltpu.VMEM((hw, tq), _F32)] * 2,
        compiler_params=_params(2),
        name="diff_attention",
    )(q_t, k, v_t, lam_params, subln)


def _dft_cos_sin(n):
    kt = np.outer(np.arange(n), np.arange(n)) % n
    ang = 2.0 * np.pi * kt / n
    return np.cos(ang), np.sin(ang)


def _channel_dft(z, cs, scale, store):
    p = z.shape[0] // 2
    gd = cs.shape[1]
    zb = z.astype(_BF16)
    for g in range(z.shape[1] // gd):
        cols = slice(g * gd, (g + 1) * gd)
        zc = jnp.concatenate([zb[:p, cols], zb[p:, cols]], axis=1)
        y = jnp.dot(zc, cs, preferred_element_type=_F32) * scale
        store(cols, y.astype(_BF16))


def _fft_stage1_kernel(a_ref, f1_ref, tw_ref, o_ref, *, d):
    n1 = a_ref.shape[0]
    f1 = f1_ref[...].astype(_BF16)
    for tt in range(a_ref.shape[1] // d):
        cols = slice(tt * d, (tt + 1) * d)
        a = jnp.dot(f1, a_ref[:, cols], preferred_element_type=_F32)
        ar, ai = a[:n1, :], a[n1:, :]
        twr, twi = tw_ref[0, :, tt:tt + 1], tw_ref[1, :, tt:tt + 1]
        o_ref[0, :, cols] = (ar * twr - ai * twi).astype(_BF16)
        o_ref[1, :, cols] = (ar * twi + ai * twr).astype(_BF16)


def _fft_stage2_kernel(b_ref, g2_ref, cs_ref, y_ref, g2_bf, cs_bf, *, scale):
    @pl.when(pl.program_id(0) == 0)
    def _():
        g2_bf[...] = g2_ref[...].astype(_BF16)
        cs_bf[...] = cs_ref[...].astype(_BF16)

    d = b_ref.shape[3]
    for kk in range(b_ref.shape[1]):
        bcat = jnp.concatenate([b_ref[0, kk], b_ref[1, kk]], axis=0)
        z = jnp.dot(g2_bf[...], bcat, preferred_element_type=_F32)

        def store(cols, y, kk=kk):
            y_ref[:, kk * d + cols.start:kk * d + cols.stop] = y

        _channel_dft(z, cs_bf[...], scale, store)


def _ctx_dft_kernel(a_ref, fc_ref, cs_ref, y_ref, *, scale):
    z = jnp.dot(fc_ref[...].astype(_BF16), a_ref[...], preferred_element_type=_F32)

    def store(cols, y):
        y_ref[:, cols] = y

    _channel_dft(z, cs_ref[...].astype(_BF16), scale, store)


def _fourier_mix(h, seq, ctx):
    rows, d = h.shape
    gd = d // FNET_GROUPS
    n2 = FFT_MINOR
    n1 = seq // n2
    g2 = FFT_STAGE1_POSITIONS
    kb = FFT_STAGE2_FREQS
    assert seq % n2 == 0 and rows % n2 == 0 and n2 % g2 == 0 and n1 % kb == 0

    c1, s1 = _dft_cos_sin(n1)
    f1 = jnp.asarray(np.concatenate([c1, -s1], axis=0), _F32)
    ang = 2.0 * np.pi * np.outer(np.arange(n1), np.arange(n2)) / seq
    tw = np.stack([np.cos(ang), -np.sin(ang)])
    tw = jnp.asarray(tw.reshape(2, n1, n2 // g2, g2).transpose(2, 0, 1, 3), _F32)
    c2, s2 = _dft_cos_sin(n2)
    g2m = jnp.asarray(np.block([[c2, s2], [-s2, c2]]), _F32)
    cc, sc = _dft_cos_sin(gd)
    cs = jnp.asarray(np.concatenate([cc, sc], axis=0), _F32)
    cx, sx = _dft_cos_sin(ctx)
    fc = jnp.asarray(np.concatenate([cx, -sx], axis=0), _F32)

    b = pl.pallas_call(
        functools.partial(_fft_stage1_kernel, d=d),
        grid=(n2 // g2,),
        in_specs=[pl.BlockSpec((n1, g2 * d), lambda c: (0, c)),
                  pl.BlockSpec((2 * n1, n1), lambda c: (0, 0)),
                  pl.BlockSpec((None, 2, n1, g2), lambda c: (c, 0, 0, 0))],
        out_specs=pl.BlockSpec((2, n1, g2 * d), lambda c: (0, 0, c)),
        out_shape=jax.ShapeDtypeStruct((2, n1, n2 * d), _BF16),
        compiler_params=_params(1),
        name="fft_stage1",
    )(h.reshape(rows // n2, n2 * d), f1, tw)

    y_lat = pl.pallas_call(
        functools.partial(_fft_stage2_kernel, scale=1.0 / math.sqrt(seq * gd)),
        grid=(n1 // kb,),
        in_specs=[pl.BlockSpec((2, kb, n2, d), lambda k: (0, k, 0, 0)),
                  pl.BlockSpec((2 * n2, 2 * n2), lambda k: (0, 0)),
                  pl.BlockSpec((2 * gd, gd), lambda k: (0, 0))],
        out_specs=pl.BlockSpec((n2, kb * d), lambda k: (0, k)),
        out_shape=jax.ShapeDtypeStruct((n2, n1 * d), _BF16),
        scratch_shapes=[pltpu.VMEM((2 * n2, 2 * n2), _BF16), pltpu.VMEM((2 * gd, gd), _BF16)],
        compiler_params=_params(1),
        name="fft_stage2",
    )(b.reshape(2, n1, n2, d), g2m, cs)

    y_ctx = pl.pallas_call(
        functools.partial(_ctx_dft_kernel, scale=1.0 / math.sqrt(ctx * gd)),
        grid=(1,),
        in_specs=[pl.BlockSpec((ctx, d), lambda c: (seq // ctx, 0)),
                  pl.BlockSpec((2 * ctx, ctx), lambda c: (0, 0)),
                  pl.BlockSpec((2 * gd, gd), lambda c: (0, 0))],
        out_specs=pl.BlockSpec((ctx, d), lambda c: (0, 0)),
        out_shape=jax.ShapeDtypeStruct((ctx, d), _BF16),
        compiler_params=_params(1),
        name="ctx_dft",
    )(h, fc, cs)
    return jnp.concatenate([y_lat.reshape(seq, d), y_ctx], axis=0)


def _rope_tables(seq, ctx):
    quarter = HEAD_DIM // 4
    t = np.arange(seq)
    inv_freq = ROPE_THETA ** (-np.arange(quarter, dtype=np.float64) / quarter)
    ang_r = (t // GRID_W)[:, None] * inv_freq[None, :]
    ang_c = (t % GRID_W)[:, None] * inv_freq[None, :]
    cos = np.concatenate([np.cos(ang_r)] * 2 + [np.cos(ang_c)] * 2, axis=1)
    sin = np.concatenate([-np.sin(ang_r), np.sin(ang_r), -np.sin(ang_c), np.sin(ang_c)], axis=1)
    cos = np.concatenate([cos, np.ones((ctx, HEAD_DIM))], axis=0)
    sin = np.concatenate([sin, np.zeros((ctx, HEAD_DIM))], axis=0)
    return jnp.asarray(cos, _F32), jnp.asarray(sin, _F32)


def kernel(x, c, ctx, c_ctx, ada_w, ada_b, ln_g, ln_b, gqa_w_qkv, gqa_q_norm, gqa_k_norm, gqa_w_o,
           diff_w_qkv, diff_lambda, diff_subln, diff_w_o, fnet_w, fnet_b,
           ffn_w_up, ffn_conv_w, ffn_conv_b, ffn_w_down):
    batch, seq, d = x.shape
    n_ctx = ctx.shape[1]
    depth = ada_w.shape[0]
    assert batch == 1 and seq % ROW_GROUP == 0 and n_ctx % ROW_GROUP == 0 and seq % GRID_W == 0
    alpha = (2 * depth) ** 0.25
    q_scale = HEAD_DIM ** -0.5 * _LOG2E

    cvec = jnp.concatenate([c, c_ctx[None, :], jnp.zeros((6, d), _F32)], axis=0)
    mods = _ada_mods(cvec, ada_w, ada_b)
    xs = jnp.concatenate([x[0], ctx[0]], axis=0)
    rows = seq + n_ctx
    rope_tabs = _rope_tables(seq, n_ctx)
    ln_g2 = ln_g.reshape(depth * 2, 1, d)
    ln_b2 = ln_b.reshape(depth * 2, 1, d)
    conv_b3 = ffn_conv_b.reshape(depth, 1, -1)
    gqa_q_norm3 = gqa_q_norm.reshape(-1, 1, HEAD_DIM)
    gqa_k_norm3 = gqa_k_norm.reshape(-1, 1, HEAD_DIM)
    diff_subln3 = diff_subln.reshape(-1, 1, 2 * HEAD_DIM)
    fnet_b3 = fnet_b.reshape(-1, 1, d)

    h = _modulate(xs, seq, mods)
    counts = [0, 0, 0]
    for i in range(depth):
        last = i == depth - 1
        kind = i % N_MIXERS
        j = counts[kind]
        counts[kind] += 1
        if kind == 0:
            n_q = d
            n_kv = d // GQA_GROUP
            q_t = _head_proj(h, gqa_w_qkv, j, 0, n_q, transpose_out=True, rope_tabs=rope_tabs,
                             gain=gqa_q_norm3, out_scale=q_scale)
            k = _head_proj(h, gqa_w_qkv, j, n_q, n_kv, transpose_out=False, rope_tabs=rope_tabs, gain=gqa_k_norm3)
            v_t = _head_proj(h, gqa_w_qkv, j, n_q + n_kv, n_kv, transpose_out=True)
            mixed, w_mix, bias = _gqa_attention(q_t, k, v_t, seq, n_ctx), gqa_w_o, None
        elif kind == 1:
            lambda_init = 0.8 - 0.6 * math.exp(-0.3 * i)
            q_t = _head_proj(h, diff_w_qkv, j, 0, d, transpose_out=True, rope_tabs=rope_tabs, out_scale=q_scale)
            k = _head_proj(h, diff_w_qkv, j, d, d, transpose_out=False, rope_tabs=rope_tabs)
            v_t = _head_proj(h, diff_w_qkv, j, 2 * d, d, transpose_out=True)
            mixed = _diff_attention(q_t, k, v_t, diff_lambda, diff_subln3, j, seq, n_ctx, lambda_init)
            w_mix, bias = diff_w_o, None
        else:
            mixed, w_mix, bias = _fourier_mix(h, seq, n_ctx), fnet_w, fnet_b3
        if _side_cast_fits(ffn_w_down.shape[1], rows):
            xs, h, w_down = _resid_ln(mixed, _weight_bf16(w_mix, j), bias, j, xs, rows, seq, mods, i, 2, ln_g2,
                                      ln_b2, 2 * i, i, 3, alpha, emit_h=True, cast_src=ffn_w_down, cast_layer=i)
        else:
            xs, h = _resid_ln(mixed, _weight_bf16(w_mix, j), bias, j, xs, rows, seq, mods, i, 2, ln_g2, ln_b2,
                              2 * i, i, 3, alpha, emit_h=True)
            w_down = _weight_bf16(ffn_w_down, i)
        a = _ffn_up(h, ffn_w_up, ffn_conv_w, conv_b3, i, seq)
        if last:
            (xs,) = _resid_ln(a, w_down, None, 0, xs, seq, seq, mods, i, 5, ln_g2, ln_b2, 2 * i + 1,
                              0, 0, alpha, emit_h=False)
        else:
            xs, h = _resid_ln(a, w_down, None, 0, xs, rows, seq, mods, i, 5, ln_g2, ln_b2, 2 * i + 1,
                              i + 1, 0, alpha, emit_h=True)
    return xs[None]
```

```python
import functools
import math

import numpy as np
import jax
import jax.numpy as jnp
from jax import lax
from jax.experimental import pallas as pl
from jax.experimental.pallas import tpu as pltpu

HEAD_DIM = 128
GQA_GROUP = 4
GRID_W = 64
N_MIXERS = 3
FNET_GROUPS = 4
ROPE_THETA = 10000.0
EPS = 1e-6
ROW_GROUP = 256
FFT_MINOR = 128
FFT_STAGE1_POSITIONS = 16
FFT_STAGE2_FREQS = 2
BF16_SUBLANES = 16
LN_ROW_CHUNK = 32
LANES = 128
ROW_TILE_CAP = 1024
COL_TILE_CAP = 1024
FFN_COL_TILE_CAP = 512
CAST_ROW_TILE_CAP = 512
V7X_VMEM_LIMIT_BYTES = 56 * 1024 * 1024

_F32 = jnp.float32
_BF16 = jnp.bfloat16


def _params(n_grid_axes):
    return pltpu.CompilerParams(dimension_semantics=("arbitrary",) * n_grid_axes,
                                vmem_limit_bytes=V7X_VMEM_LIMIT_BYTES)


def _largest_tile(n, unit, cap):
    best = None
    for t in range(unit, min(n, cap) + 1, unit):
        if n % t == 0:
            best = t
    assert best is not None, (n, unit, cap)
    return best


def _silu(x):
    return x * (1.0 / (1.0 + jnp.exp(-x)))


def _ada_kernel(cv_ref, w_ref, b_ref, o_ref):
    s = _silu(cv_ref[...]).astype(_BF16)
    acc = jnp.dot(s, w_ref[...].astype(_BF16), preferred_element_type=_F32)
    o_ref[...] = acc + b_ref[...]


def _ada_mods(cvec, ada_w, ada_b):
    depth, d, n = ada_w.shape
    tn = _largest_tile(n, LANES, COL_TILE_CAP)
    return pl.pallas_call(
        _ada_kernel,
        grid=(depth, n // tn),
        in_specs=[pl.BlockSpec((8, d), lambda l, j: (0, 0)),
                  pl.BlockSpec((None, d, tn), lambda l, j: (l, 0, j)),
                  pl.BlockSpec((None, 1, tn), lambda l, j: (l, 0, j))],
        out_specs=pl.BlockSpec((None, 8, tn), lambda l, j: (l, 0, j)),
        out_shape=jax.ShapeDtypeStruct((depth, 8, n), _F32),
        compiler_params=_params(2),
        name="ada_mods",
    )(cvec, ada_w, ada_b.reshape(depth, 1, n))


def _modulate_kernel(x_ref, sh_ref, sc_ref, h_ref, *, n_sub, n_lat_groups):
    i = pl.program_id(0)
    for s in range(n_sub):
        rows = slice(s * ROW_GROUP, (s + 1) * ROW_GROUP)
        is_ctx = (i * n_sub + s) >= n_lat_groups
        sh = jnp.where(is_ctx, sh_ref[1:2, :], sh_ref[0:1, :])
        sc = jnp.where(is_ctx, sc_ref[1:2, :], sc_ref[0:1, :])
        h_ref[rows, :] = (x_ref[rows, :] * (1.0 + sc) + sh).astype(_BF16)


def _modulate(xs, seq, mods):
    rows, d = xs.shape
    tm = _largest_tile(rows, ROW_GROUP, ROW_TILE_CAP)
    return pl.pallas_call(
        functools.partial(_modulate_kernel, n_sub=tm // ROW_GROUP, n_lat_groups=seq // ROW_GROUP),
        grid=(rows // tm,),
        in_specs=[pl.BlockSpec((tm, d), lambda i: (i, 0)),
                  pl.BlockSpec((None, 8, d), lambda i: (0, 0, 0)),
                  pl.BlockSpec((None, 8, d), lambda i: (0, 0, 1))],
        out_specs=pl.BlockSpec((tm, d), lambda i: (i, 0)),
        out_shape=jax.ShapeDtypeStruct((rows, d), _BF16),
        compiler_params=_params(1),
        name="modulate",
    )(xs, mods, mods)


def _rope(y, cos, sin_signed):
    lane = lax.broadcasted_iota(jnp.int32, y.shape, 1)
    partner = jnp.where((lane % 64) < 32, pltpu.roll(y, 96, 1), pltpu.roll(y, 32, 1))
    return y * cos + partner * sin_signed


def _lagged_steps(i, n_tiles, produce, consume):
    @pl.when(jnp.logical_and(i < n_tiles, i % 2 == 0))
    def _():
        consume(1)
        produce(0)

    @pl.when(jnp.logical_and(i < n_tiles, i % 2 == 1))
    def _():
        consume(0)
        produce(1)

    @pl.when(i == n_tiles)
    def _():
        consume((n_tiles - 1) % 2)


def _lagged_index_maps(n_tiles):
    return (lambda i: jnp.minimum(i, n_tiles - 1)), (lambda i: jnp.maximum(i - 1, 0))


def _head_proj_kernel(*refs, n_tiles, rope, transpose_out, has_norm, out_scale):
    it = iter(refs)
    h_ref, w_ref = next(it), next(it)
    gain_ref = next(it) if has_norm else None
    cos_ref = next(it) if rope else None
    sin_ref = next(it) if rope else None
    o_ref, wbf_ref, raw_ref = next(it), next(it), next(it)
    i = pl.program_id(1)

    @pl.when(i == 0)
    def _():
        wbf_ref[...] = w_ref[...].astype(_BF16)
        raw_ref[1] = jnp.zeros(raw_ref.shape[1:], _F32)

    def produce(slot):
        raw_ref[slot] = jnp.dot(h_ref[...], wbf_ref[...], preferred_element_type=_F32)

    def consume(slot):
        for hh in range(raw_ref.shape[2] // HEAD_DIM):
            cols = slice(hh * HEAD_DIM, (hh + 1) * HEAD_DIM)
            y = raw_ref[slot, :, cols]
            if has_norm:
                ms = jnp.mean(y * y, axis=-1, keepdims=True)
                y = y * lax.rsqrt(ms + EPS) * gain_ref[...]
            if rope:
                y = _rope(y, cos_ref[...], sin_ref[...])
            if out_scale != 1.0:
                y = y * out_scale
            if transpose_out:
                o_ref[cols, :] = y.T.astype(_BF16)
            else:
                o_ref[:, cols] = y.astype(_BF16)

    _lagged_steps(i, n_tiles, produce, consume)


def _head_proj(h, w, w_layer, col0, n_cols, *, transpose_out, rope_tabs=None, gain=None, out_scale=1.0):
    rows, d = h.shape
    tm = _largest_tile(rows, ROW_GROUP, ROW_TILE_CAP)
    tn = _largest_tile(math.gcd(n_cols, col0) if col0 else n_cols, HEAD_DIM, COL_TILE_CAP)
    j0 = col0 // tn
    n_tiles = rows // tm
    produced, consumed = _lagged_index_maps(n_tiles)
    in_specs = [pl.BlockSpec((tm, d), lambda j, i: (produced(i), 0)),
                pl.BlockSpec((None, d, tn), lambda j, i: (w_layer, 0, j0 + j))]
    args = [h, w]
    if gain is not None:
        in_specs.append(pl.BlockSpec((None, 1, HEAD_DIM), lambda j, i: (w_layer, 0, 0)))
        args.append(gain)
    if rope_tabs is not None:
        in_specs += [pl.BlockSpec((tm, HEAD_DIM), lambda j, i: (consumed(i), 0))] * 2
        args += list(rope_tabs)
    if transpose_out:
        out_spec = pl.BlockSpec((tn, tm), lambda j, i: (j, consumed(i)))
        out_shape = jax.ShapeDtypeStruct((n_cols, rows), _BF16)
    else:
        out_spec = pl.BlockSpec((tm, tn), lambda j, i: (consumed(i), j))
        out_shape = jax.ShapeDtypeStruct((rows, n_cols), _BF16)
    return pl.pallas_call(
        functools.partial(_head_proj_kernel, n_tiles=n_tiles, rope=rope_tabs is not None,
                          transpose_out=transpose_out, has_norm=gain is not None, out_scale=out_scale),
        grid=(n_cols // tn, n_tiles + 1),
        in_specs=in_specs, out_specs=out_spec, out_shape=out_shape,
        scratch_shapes=[pltpu.VMEM((d, tn), _BF16), pltpu.VMEM((2, tm, tn), _F32)],
        compiler_params=_params(2),
        name="head_proj",
    )(*args)


def _cast_kernel(w_ref, o_ref):
    o_ref[...] = w_ref[...].astype(_BF16)


def _weight_bf16(w, w_layer):
    _, k, n = w.shape
    tk = _largest_tile(k, BF16_SUBLANES, CAST_ROW_TILE_CAP)
    return pl.pallas_call(
        _cast_kernel,
        grid=(k // tk,),
        in_specs=[pl.BlockSpec((None, tk, n), lambda r: (w_layer, r, 0))],
        out_specs=pl.BlockSpec((tk, n), lambda r: (r, 0)),
        out_shape=jax.ShapeDtypeStruct((k, n), _BF16),
        compiler_params=_params(1),
        name="weight_bf16",
    )(w)


def _side_cast_fits(k_rows, n_rows):
    return k_rows // _largest_tile(k_rows, BF16_SUBLANES, CAST_ROW_TILE_CAP) <= n_rows // ROW_GROUP


def _resid_ln_kernel(*refs, n_tiles, alpha, has_bias, emit_h, n_cast, n_lat_groups):
    it = iter(refs)
    a_ref, w_ref = next(it), next(it)
    b_ref = next(it) if has_bias else None
    x_ref, gate_ref, g_ref, be_ref = next(it), next(it), next(it), next(it)
    sh_ref = next(it) if emit_h else None
    sc_ref = next(it) if emit_h else None
    cast_in_ref = next(it) if n_cast else None
    xo_ref = next(it)
    h_ref = next(it) if emit_h else None
    cast_out_ref = next(it) if n_cast else None
    raw_ref = next(it)
    i = pl.program_id(0)
    tm = raw_ref.shape[1]

    @pl.when(i == 0)
    def _():
        raw_ref[1] = jnp.zeros(raw_ref.shape[1:], _F32)

    if n_cast:
        @pl.when(i < n_cast)
        def _():
            cast_out_ref[...] = cast_in_ref[...].astype(_BF16)

    def produce(slot):
        raw_ref[slot] = jnp.dot(a_ref[...], w_ref[...], preferred_element_type=_F32)

    def consume(slot):
        is_ctx = (i - 1) >= n_lat_groups
        gate = jnp.where(is_ctx, gate_ref[1:2, :], gate_ref[0:1, :])
        if emit_h:
            sh = jnp.where(is_ctx, sh_ref[1:2, :], sh_ref[0:1, :])
            sc1 = 1.0 + jnp.where(is_ctx, sc_ref[1:2, :], sc_ref[0:1, :])
        for r0 in range(0, tm, LN_ROW_CHUNK):
            rows = slice(r0, r0 + LN_ROW_CHUNK)
            acc = raw_ref[slot, rows, :]
            if has_bias:
                acc = acc + b_ref[...]
            z = alpha * x_ref[rows, :] + gate * acc
            mu = jnp.mean(z, axis=-1, keepdims=True)
            zc = z - mu
            var = jnp.mean(zc * zc, axis=-1, keepdims=True)
            x = zc * lax.rsqrt(var + EPS) * g_ref[...] + be_ref[...]
            xo_ref[rows, :] = x
            if emit_h:
                h_ref[rows, :] = (x * sc1 + sh).astype(_BF16)

    _lagged_steps(i, n_tiles, produce, consume)


def _resid_ln(a, w_bf, bias, bias_layer, xs, n_rows, seq, mods, mod_layer, gate_idx, ln_g2, ln_b2, ln_idx,
              next_layer, next_shift_idx, alpha, *, emit_h, cast_src=None, cast_layer=0):
    k, d = w_bf.shape
    tm = ROW_GROUP
    n_tiles = n_rows // tm
    produced, consumed = _lagged_index_maps(n_tiles)
    n_cast = 0
    if cast_src is not None:
        _, kc, dc = cast_src.shape
        tkc = _largest_tile(kc, BF16_SUBLANES, CAST_ROW_TILE_CAP)
        n_cast = kc // tkc
        assert _side_cast_fits(kc, n_rows), (n_cast, n_tiles)
    in_specs = [pl.BlockSpec((tm, k), lambda i: (produced(i), 0)),
                pl.BlockSpec((k, d), lambda i: (0, 0), pipeline_mode=pl.Buffered(1))]
    args = [a, w_bf]
    if bias is not None:
        in_specs.append(pl.BlockSpec((None, 1, d), lambda i: (bias_layer, 0, 0)))
        args.append(bias)
    in_specs += [pl.BlockSpec((tm, d), lambda i: (consumed(i), 0)),
                 pl.BlockSpec((None, 8, d), lambda i: (mod_layer, 0, gate_idx)),
                 pl.BlockSpec((None, 1, d), lambda i: (ln_idx, 0, 0)),
                 pl.BlockSpec((None, 1, d), lambda i: (ln_idx, 0, 0))]
    args += [xs, mods, ln_g2, ln_b2]
    out_specs = [pl.BlockSpec((tm, d), lambda i: (consumed(i), 0))]
    out_shape = [jax.ShapeDtypeStruct((n_rows, d), _F32)]
    if emit_h:
        in_specs += [pl.BlockSpec((None, 8, d), lambda i: (next_layer, 0, next_shift_idx)),
                     pl.BlockSpec((None, 8, d), lambda i: (next_layer, 0, next_shift_idx + 1))]
        args += [mods, mods]
        out_specs.append(pl.BlockSpec((tm, d), lambda i: (consumed(i), 0)))
        out_shape.append(jax.ShapeDtypeStruct((n_rows, d), _BF16))
    if n_cast:
        in_specs.append(pl.BlockSpec((None, tkc, dc), lambda i: (cast_layer, jnp.minimum(i, n_cast - 1), 0)))
        args.append(cast_src)
        out_specs.append(pl.BlockSpec((tkc, dc), lambda i: (jnp.minimum(i, n_cast - 1), 0)))
        out_shape.append(jax.ShapeDtypeStruct((kc, dc), _BF16))
    return pl.pallas_call(
        functools.partial(_resid_ln_kernel, n_tiles=n_tiles, alpha=alpha, has_bias=bias is not None,
                          emit_h=emit_h, n_cast=n_cast, n_lat_groups=seq // ROW_GROUP),
        grid=(n_tiles + 1,),
        in_specs=in_specs, out_specs=out_specs, out_shape=out_shape,
        scratch_shapes=[pltpu.VMEM((2, tm, d), _F32)],
        compiler_params=_params(1),
        name="resid_ln",
    )(*args)


def _ffn_up_kernel(hp_ref, hm_ref, hn_ref, wv_ref, wg_ref, cwv_ref, cwg_ref, cbv_ref, cbg_ref, o_ref,
                   wv_bf, wg_bf, hs_ref, *, seq, rows):
    i = pl.program_id(1)
    tm = hm_ref.shape[0]
    halo = BF16_SUBLANES

    @pl.when(i == 0)
    def _():
        wv_bf[...] = wv_ref[...].astype(_BF16)
        wg_bf[...] = wg_ref[...].astype(_BF16)

    hs_ref[0:halo, :] = hp_ref[...]
    hs_ref[halo:halo + tm, :] = hm_ref[...]
    hs_ref[halo + tm:, :] = hn_ref[...]
    hs = hs_ref[...]
    r = i * tm + lax.broadcasted_iota(jnp.int32, (tm, 1), 0)
    keep_prev = jnp.where(r == 0, 0.0, 1.0) * jnp.where(r == seq, 0.0, 1.0)
    keep_next = jnp.where(r == seq - 1, 0.0, 1.0) * jnp.where(r == rows - 1, 0.0, 1.0)

    def conv(u, cw_ref, cb_ref):
        cw = cw_ref[...]
        return (cw[1:2, :] * u[halo:halo + tm, :]
                + keep_prev * (cw[0:1, :] * u[halo - 1:halo - 1 + tm, :])
                + keep_next * (cw[2:3, :] * u[halo + 1:halo + 1 + tm, :])
                + cb_ref[...])

    u_gate = jnp.dot(hs, wg_bf[...], preferred_element_type=_F32)
    u_val = jnp.dot(hs, wv_bf[...], preferred_element_type=_F32)
    gate = _silu(conv(u_gate, cwg_ref, cbg_ref))
    o_ref[...] = (gate * conv(u_val, cwv_ref, cbv_ref)).astype(_BF16)


def _ffn_up(h, w_up, conv_w, conv_b3, layer, seq):
    rows, d = h.shape
    d_ff = w_up.shape[2] // 2
    tm = _largest_tile(rows, ROW_GROUP, ROW_TILE_CAP)
    tf = _largest_tile(d_ff, LANES, FFN_COL_TILE_CAP)
    nf = d_ff // tf
    halo = BF16_SUBLANES
    per_tile = tm // halo
    last_halo_blk = rows // halo - 1
    return pl.pallas_call(
        functools.partial(_ffn_up_kernel, seq=seq, rows=rows),
        grid=(nf, rows // tm),
        in_specs=[pl.BlockSpec((halo, d), lambda j, i: (jnp.maximum(i * per_tile - 1, 0), 0)),
                  pl.BlockSpec((tm, d), lambda j, i: (i, 0)),
                  pl.BlockSpec((halo, d), lambda j, i: (jnp.minimum((i + 1) * per_tile, last_halo_blk), 0)),
                  pl.BlockSpec((None, d, tf), lambda j, i: (layer, 0, j)),
                  pl.BlockSpec((None, d, tf), lambda j, i: (layer, 0, nf + j)),
                  pl.BlockSpec((None, 3, tf), lambda j, i: (layer, 0, j)),
                  pl.BlockSpec((None, 3, tf), lambda j, i: (layer, 0, nf + j)),
                  pl.BlockSpec((None, 1, tf), lambda j, i: (layer, 0, j)),
                  pl.BlockSpec((None, 1, tf), lambda j, i: (layer, 0, nf + j))],
        out_specs=pl.BlockSpec((tm, tf), lambda j, i: (i, j)),
        out_shape=jax.ShapeDtypeStruct((rows, d_ff), _BF16),
        scratch_shapes=[pltpu.VMEM((d, tf), _BF16), pltpu.VMEM((d, tf), _BF16),
                        pltpu.VMEM((tm + 2 * halo, d), _BF16)],
        compiler_params=_params(2),
        name="ffn_up",
    )(h, h, h, w_up, w_up, conv_w, conv_w, conv_b3, conv_b3)


_NEG_BIG = -1e30
_LOG2E = math.log2(math.e)
ATTN_KEY_CHUNK_CAP = 1024


def _key_chunk(seq, ctx):
    assert ctx % 128 == 0, ctx
    return _largest_tile(seq, LANES, ATTN_KEY_CHUNK_CAP)


def _attend_t(qt_list, k_slices, vt_slice, acc_refs, s_ref, sc_ref, *, latent, seq, ctx, tk):
    n_maps = len(qt_list)
    tq = qt_list[0].shape[1]

    def scores(rows, c):
        return jnp.dot(k_slices[c](rows), qt_list[c], preferred_element_type=_F32)

    def absorb(s, vt, c, m, l):
        m_new = jnp.maximum(m, jnp.max(s, axis=0, keepdims=True))
        p = jnp.exp2(s - m_new)
        alpha = jnp.exp2(m - m_new)
        acc_refs[c][...] = alpha * acc_refs[c][...] + jnp.dot(vt, p.astype(_BF16), preferred_element_type=_F32)
        return m_new, alpha * l + jnp.sum(p, axis=0, keepdims=True)

    for acc_ref in acc_refs:
        acc_ref[...] = jnp.zeros(acc_ref.shape, _F32)
    carry = tuple((jnp.full((1, tq), _NEG_BIG, _F32), jnp.zeros((1, tq), _F32)) for _ in range(n_maps))
    ctx_rows = slice(seq, seq + ctx)
    n_chunks = seq // tk if latent else 0
    for c in range(n_maps):
        if latent:
            s_ref[0, c] = scores(slice(0, tk), c)
        else:
            sc_ref[c] = scores(ctx_rows, c)
    for n in range(n_chunks):
        for c in range(n_maps):
            if n + 1 < n_chunks:
                s_ref[(n + 1) % 2, c] = scores(slice((n + 1) * tk, (n + 2) * tk), c)
            else:
                sc_ref[c] = scores(ctx_rows, c)
        vt = vt_slice(slice(n * tk, (n + 1) * tk))
        carry = tuple(absorb(s_ref[n % 2, c], vt, c, *carry[c]) for c in range(n_maps))
    vt_ctx = vt_slice(ctx_rows)
    carry = tuple(absorb(sc_ref[c], vt_ctx, c, *carry[c]) for c in range(n_maps))
    return [l for (_, l) in carry]


def _per_query_block(i, n_latent_blocks, run):
    @pl.when(i < n_latent_blocks)
    def _():
        run(True)

    @pl.when(i >= n_latent_blocks)
    def _():
        run(False)


def _gqa_attn_kernel(qt_ref, k_ref, vt_ref, o_ref, s_ref, sc_ref, *acc_refs, seq, ctx, tk):
    tq = qt_ref.shape[1]

    def run(latent):
        qt_list = [qt_ref[g * HEAD_DIM:(g + 1) * HEAD_DIM, :] for g in range(GQA_GROUP)]
        k_slices = [lambda rows: k_ref[rows, :]] * GQA_GROUP
        sums = _attend_t(qt_list, k_slices, lambda rows: vt_ref[:, rows], acc_refs, s_ref, sc_ref,
                         latent=latent, seq=seq, ctx=ctx, tk=tk)
        for g in range(GQA_GROUP):
            o_t = acc_refs[g][...] * (1.0 / sums[g])
            o_ref[:, g * HEAD_DIM:(g + 1) * HEAD_DIM] = o_t.T.astype(_BF16)

    _per_query_block(pl.program_id(1), seq // tq, run)


def _gqa_attention(q_t, k, v_t, seq, ctx):
    dq, rows = q_t.shape
    kvh = k.shape[1] // HEAD_DIM
    tq = ROW_GROUP
    qw = GQA_GROUP * HEAD_DIM
    tk = _key_chunk(seq, ctx)
    return pl.pallas_call(
        functools.partial(_gqa_attn_kernel, seq=seq, ctx=ctx, tk=tk),
        grid=(kvh, rows // tq),
        in_specs=[pl.BlockSpec((qw, tq), lambda g, i: (g, i)),
                  pl.BlockSpec((rows, HEAD_DIM), lambda g, i: (0, g)),
                  pl.BlockSpec((HEAD_DIM, rows), lambda g, i: (g, 0))],
        out_specs=pl.BlockSpec((tq, qw), lambda g, i: (i, g)),
        out_shape=jax.ShapeDtypeStruct((rows, dq), _BF16),
        scratch_shapes=[pltpu.VMEM((2, GQA_GROUP, tk, tq), _F32), pltpu.VMEM((GQA_GROUP, ctx, tq), _F32)]
        + [pltpu.VMEM((HEAD_DIM, tq), _F32)] * GQA_GROUP,
        compiler_params=_params(2),
        name="gqa_attention",
    )(q_t, k, v_t)


def _diff_attn_kernel(qt_ref, k_ref, vt_ref, lam_ref, sub_ref, o_ref, s_ref, sc_ref, *acc_refs,
                      seq, ctx, tk, lambda_init):
    tq = qt_ref.shape[1]

    def run(latent):
        qt_list = [qt_ref[c * HEAD_DIM:(c + 1) * HEAD_DIM, :] for c in range(2)]
        k_slices = [functools.partial(lambda rows, c: k_ref[rows, c * HEAD_DIM:(c + 1) * HEAD_DIM], c=c)
                    for c in range(2)]
        l0, l1 = _attend_t(qt_list, k_slices, lambda rows: vt_ref[:, rows], acc_refs, s_ref, sc_ref,
                           latent=latent, seq=seq, ctx=ctx, tk=tk)
        lp = lam_ref[...]
        lam = (jnp.exp(jnp.sum(lp[0:1, :] * lp[1:2, :], axis=1, keepdims=True))
               - jnp.exp(jnp.sum(lp[2:3, :] * lp[3:4, :], axis=1, keepdims=True)) + lambda_init)
        o_t = acc_refs[0][...] * (1.0 / l0) - lam * (acc_refs[1][...] * (1.0 / l1))
        o = o_t.T
        ms = jnp.mean(o * o, axis=-1, keepdims=True)
        o = o * lax.rsqrt(ms + EPS) * sub_ref[...] * (1.0 - lambda_init)
        o_ref[...] = o.astype(_BF16)

    _per_query_block(pl.program_id(1), seq // tq, run)


def _diff_attention(q_t, k, v_t, lam_params, subln, layer_j, seq, ctx, lambda_init):
    d, rows = q_t.shape
    hw = 2 * HEAD_DIM
    tq = ROW_GROUP
    tk = _key_chunk(seq, ctx)
    return pl.pallas_call(
        functools.partial(_diff_attn_kernel, seq=seq, ctx=ctx, tk=tk, lambda_init=lambda_init),
        grid=(d // hw, rows // tq),
        in_specs=[pl.BlockSpec((hw, tq), lambda h, i: (h, i)),
                  pl.BlockSpec((rows, hw), lambda h, i: (0, h)),
                  pl.BlockSpec((hw, rows), lambda h, i: (h, 0)),
                  pl.BlockSpec((None, 4, HEAD_DIM), lambda h, i: (layer_j, 0, 0)),
                  pl.BlockSpec((None, 1, hw), lambda h, i: (layer_j, 0, 0))],
        out_specs=pl.BlockSpec((tq, hw), lambda h, i: (i, h)),
        out_shape=jax.ShapeDtypeStruct((rows, d), _BF16),
        scratch_shapes=[pltpu.VMEM((2, 2, tk, tq), _F32), pltpu.VMEM((2, ctx, tq), _F32)]
        + [pltpu.VMEM((hw, tq), _F32)] * 2,
        compiler_params=_params(2),
        name="diff_attention",
    )(q_t, k, v_t, lam_params, subln)


def _dft_cos_sin(n):
    kt = np.outer(np.arange(n), np.arange(n)) % n
    ang = 2.0 * np.pi * kt / n
    return np.cos(ang), np.sin(ang)


def _channel_dft(z, cs, scale, store):
    p = z.shape[0] // 2
    gd = cs.shape[1]
    zb = z.astype(_BF16)
    for g in range(z.shape[1] // gd):
        cols = slice(g * gd, (g + 1) * gd)
        zc = jnp.concatenate([zb[:p, cols], zb[p:, cols]], axis=1)
        y = jnp.dot(zc, cs, preferred_element_type=_F32) * scale
        store(cols, y.astype(_BF16))


def _fft_stage1_kernel(a_ref, f1_ref, tw_ref, o_ref, *, d):
    n1 = a_ref.shape[0]
    f1 = f1_ref[...].astype(_BF16)
    for tt in range(a_ref.shape[1] // d):
        cols = slice(tt * d, (tt + 1) * d)
        a = jnp.dot(f1, a_ref[:, cols], preferred_element_type=_F32)
        ar, ai = a[:n1, :], a[n1:, :]
        twr, twi = tw_ref[0, :, tt:tt + 1], tw_ref[1, :, tt:tt + 1]
        o_ref[0, :, cols] = (ar * twr - ai * twi).astype(_BF16)
        o_ref[1, :, cols] = (ar * twi + ai * twr).astype(_BF16)


def _fft_stage2_kernel(b_ref, g2_ref, cs_ref, y_ref, g2_bf, cs_bf, *, scale):
    @pl.when(pl.program_id(0) == 0)
    def _():
        g2_bf[...] = g2_ref[...].astype(_BF16)
        cs_bf[...] = cs_ref[...].astype(_BF16)

    d = b_ref.shape[3]
    for kk in range(b_ref.shape[1]):
        bcat = jnp.concatenate([b_ref[0, kk], b_ref[1, kk]], axis=0)
        z = jnp.dot(g2_bf[...], bcat, preferred_element_type=_F32)

        def store(cols, y, kk=kk):
            y_ref[:, kk * d + cols.start:kk * d + cols.stop] = y

        _channel_dft(z, cs_bf[...], scale, store)


def _ctx_dft_kernel(a_ref, fc_ref, cs_ref, y_ref, *, scale):
    z = jnp.dot(fc_ref[...].astype(_BF16), a_ref[...], preferred_element_type=_F32)

    def store(cols, y):
        y_ref[:, cols] = y

    _channel_dft(z, cs_ref[...].astype(_BF16), scale, store)


def _fourier_mix(h, seq, ctx):
    rows, d = h.shape
    gd = d // FNET_GROUPS
    n2 = FFT_MINOR
    n1 = seq // n2
    g2 = FFT_STAGE1_POSITIONS
    kb = FFT_STAGE2_FREQS
    assert seq % n2 == 0 and rows % n2 == 0 and n2 % g2 == 0 and n1 % kb == 0

    c1, s1 = _dft_cos_sin(n1)
    f1 = jnp.asarray(np.concatenate([c1, -s1], axis=0), _F32)
    ang = 2.0 * np.pi * np.outer(np.arange(n1), np.arange(n2)) / seq
    tw = np.stack([np.cos(ang), -np.sin(ang)])
    tw = jnp.asarray(tw.reshape(2, n1, n2 // g2, g2).transpose(2, 0, 1, 3), _F32)
    c2, s2 = _dft_cos_sin(n2)
    g2m = jnp.asarray(np.block([[c2, s2], [-s2, c2]]), _F32)
    cc, sc = _dft_cos_sin(gd)
    cs = jnp.asarray(np.concatenate([cc, sc], axis=0), _F32)
    cx, sx = _dft_cos_sin(ctx)
    fc = jnp.asarray(np.concatenate([cx, -sx], axis=0), _F32)

    b = pl.pallas_call(
        functools.partial(_fft_stage1_kernel, d=d),
        grid=(n2 // g2,),
        in_specs=[pl.BlockSpec((n1, g2 * d), lambda c: (0, c)),
                  pl.BlockSpec((2 * n1, n1), lambda c: (0, 0)),
                  pl.BlockSpec((None, 2, n1, g2), lambda c: (c, 0, 0, 0))],
        out_specs=pl.BlockSpec((2, n1, g2 * d), lambda c: (0, 0, c)),
        out_shape=jax.ShapeDtypeStruct((2, n1, n2 * d), _BF16),
        compiler_params=_params(1),
        name="fft_stage1",
    )(h.reshape(rows // n2, n2 * d), f1, tw)

    y_lat = pl.pallas_call(
        functools.partial(_fft_stage2_kernel, scale=1.0 / math.sqrt(seq * gd)),
        grid=(n1 // kb,),
        in_specs=[pl.BlockSpec((2, kb, n2, d), lambda k: (0, k, 0, 0)),
                  pl.BlockSpec((2 * n2, 2 * n2), lambda k: (0, 0)),
                  pl.BlockSpec((2 * gd, gd), lambda k: (0, 0))],
        out_specs=pl.BlockSpec((n2, kb * d), lambda k: (0, k)),
        out_shape=jax.ShapeDtypeStruct((n2, n1 * d), _BF16),
        scratch_shapes=[pltpu.VMEM((2 * n2, 2 * n2), _BF16), pltpu.VMEM((2 * gd, gd), _BF16)],
        compiler_params=_params(1),
        name="fft_stage2",
    )(b.reshape(2, n1, n2, d), g2m, cs)

    y_ctx = pl.pallas_call(
        functools.partial(_ctx_dft_kernel, scale=1.0 / math.sqrt(ctx * gd)),
        grid=(1,),
        in_specs=[pl.BlockSpec((ctx, d), lambda c: (seq // ctx, 0)),
                  pl.BlockSpec((2 * ctx, ctx), lambda c: (0, 0)),
                  pl.BlockSpec((2 * gd, gd), lambda c: (0, 0))],
        out_specs=pl.BlockSpec((ctx, d), lambda c: (0, 0)),
        out_shape=jax.ShapeDtypeStruct((ctx, d), _BF16),
        compiler_params=_params(1),
        name="ctx_dft",
    )(h, fc, cs)
    return jnp.concatenate([y_lat.reshape(seq, d), y_ctx], axis=0)


def _rope_tables(seq, ctx):
    quarter = HEAD_DIM // 4
    t = np.arange(seq)
    inv_freq = ROPE_THETA ** (-np.arange(quarter, dtype=np.float64) / quarter)
    ang_r = (t // GRID_W)[:, None] * inv_freq[None, :]
    ang_c = (t % GRID_W)[:, None] * inv_freq[None, :]
    cos = np.concatenate([np.cos(ang_r)] * 2 + [np.cos(ang_c)] * 2, axis=1)
    sin = np.concatenate([-np.sin(ang_r), np.sin(ang_r), -np.sin(ang_c), np.sin(ang_c)], axis=1)
    cos = np.concatenate([cos, np.ones((ctx, HEAD_DIM))], axis=0)
    sin = np.concatenate([sin, np.zeros((ctx, HEAD_DIM))], axis=0)
    return jnp.asarray(cos, _F32), jnp.asarray(sin, _F32)


def kernel(x, c, ctx, c_ctx, ada_w, ada_b, ln_g, ln_b, gqa_w_qkv, gqa_q_norm, gqa_k_norm, gqa_w_o,
           diff_w_qkv, diff_lambda, diff_subln, diff_w_o, fnet_w, fnet_b,
           ffn_w_up, ffn_conv_w, ffn_conv_b, ffn_w_down):
    batch, seq, d = x.shape
    n_ctx = ctx.shape[1]
    depth = ada_w.shape[0]
    assert batch == 1 and seq % ROW_GROUP == 0 and n_ctx % ROW_GROUP == 0 and seq % GRID_W == 0
    alpha = (2 * depth) ** 0.25
    q_scale = HEAD_DIM ** -0.5 * _LOG2E

    cvec = jnp.concatenate([c, c_ctx[None, :], jnp.zeros((6, d), _F32)], axis=0)
    mods = _ada_mods(cvec, ada_w, ada_b)
    xs = jnp.concatenate([x[0], ctx[0]], axis=0)
    rows = seq + n_ctx
    rope_tabs = _rope_tables(seq, n_ctx)
    ln_g2 = ln_g.reshape(depth * 2, 1, d)
    ln_b2 = ln_b.reshape(depth * 2, 1, d)
    conv_b3 = ffn_conv_b.reshape(depth, 1, -1)
    gqa_q_norm3 = gqa_q_norm.reshape(-1, 1, HEAD_DIM)
    gqa_k_norm3 = gqa_k_norm.reshape(-1, 1, HEAD_DIM)
    diff_subln3 = diff_subln.reshape(-1, 1, 2 * HEAD_DIM)
    fnet_b3 = fnet_b.reshape(-1, 1, d)

    h = _modulate(xs, seq, mods)
    counts = [0, 0, 0]
    for i in range(depth):
        last = i == depth - 1
        kind = i % N_MIXERS
        j = counts[kind]
        counts[kind] += 1
        if kind == 0:
            n_q = d
            n_kv = d // GQA_GROUP
            q_t = _head_proj(h, gqa_w_qkv, j, 0, n_q, transpose_out=True, rope_tabs=rope_tabs,
                             gain=gqa_q_norm3, out_scale=q_scale)
            k = _head_proj(h, gqa_w_qkv, j, n_q, n_kv, transpose_out=False, rope_tabs=rope_tabs, gain=gqa_k_norm3)
            v_t = _head_proj(h, gqa_w_qkv, j, n_q + n_kv, n_kv, transpose_out=True)
            mixed, w_mix, bias = _gqa_attention(q_t, k, v_t, seq, n_ctx), gqa_w_o, None
        elif kind == 1:
            lambda_init = 0.8 - 0.6 * math.exp(-0.3 * i)
            q_t = _head_proj(h, diff_w_qkv, j, 0, d, transpose_out=True, rope_tabs=rope_tabs, out_scale=q_scale)
            k = _head_proj(h, diff_w_qkv, j, d, d, transpose_out=False, rope_tabs=rope_tabs)
            v_t = _head_proj(h, diff_w_qkv, j, 2 * d, d, transpose_out=True)
            mixed = _diff_attention(q_t, k, v_t, diff_lambda, diff_subln3, j, seq, n_ctx, lambda_init)
            w_mix, bias = diff_w_o, None
        else:
            mixed, w_mix, bias = _fourier_mix(h, seq, n_ctx), fnet_w, fnet_b3
        if _side_cast_fits(ffn_w_down.shape[1], rows):
            xs, h, w_down = _resid_ln(mixed, _weight_bf16(w_mix, j), bias, j, xs, rows, seq, mods, i, 2, ln_g2,
                                      ln_b2, 2 * i, i, 3, alpha, emit_h=True, cast_src=ffn_w_down, cast_layer=i)
        else:
            xs, h = _resid_ln(mixed, _weight_bf16(w_mix, j), bias, j, xs, rows, seq, mods, i, 2, ln_g2, ln_b2,
                              2 * i, i, 3, alpha, emit_h=True)
            w_down = _weight_bf16(ffn_w_down, i)
        a = _ffn_up(h, ffn_w_up, ffn_conv_w, conv_b3, i, seq)
        if last:
            (xs,) = _resid_ln(a, w_down, None, 0, xs, seq, seq, mods, i, 5, ln_g2, ln_b2, 2 * i + 1,
                              0, 0, alpha, emit_h=False)
        else:
            xs, h = _resid_ln(a, w_down, None, 0, xs, rows, seq, mods, i, 5, ln_g2, ln_b2, 2 * i + 1,
                              i + 1, 0, alpha, emit_h=True)
    return xs[None]
```

```python
import functools
import math

import numpy as np
import jax
import jax.numpy as jnp
from jax import lax
from jax.experimental import pallas as pl
from jax.experimental.pallas import tpu as pltpu

HEAD_DIM = 128
GQA_GROUP = 4
GRID_W = 64
N_MIXERS = 3
FNET_GROUPS = 4
ROPE_THETA = 10000.0
EPS = 1e-6
ROW_GROUP = 256
FFT_MINOR = 128
FFT_STAGE1_POSITIONS = 16
FFT_STAGE2_FREQS = 2
BF16_SUBLANES = 16
LN_ROW_CHUNK = 32
LANES = 128
ROW_TILE_CAP = 1024
COL_TILE_CAP = 1024
FFN_COL_TILE_CAP = 512
CAST_ROW_TILE_CAP = 256
V7X_VMEM_LIMIT_BYTES = 56 * 1024 * 1024

_F32 = jnp.float32
_BF16 = jnp.bfloat16


def _params(n_grid_axes):
    return pltpu.CompilerParams(dimension_semantics=("arbitrary",) * n_grid_axes,
                                vmem_limit_bytes=V7X_VMEM_LIMIT_BYTES)


def _largest_tile(n, unit, cap):
    best = None
    for t in range(unit, min(n, cap) + 1, unit):
        if n % t == 0:
            best = t
    assert best is not None, (n, unit, cap)
    return best


def _silu(x):
    return x * (1.0 / (1.0 + jnp.exp(-x)))


def _ada_kernel(cv_ref, w_ref, b_ref, o_ref):
    s = _silu(cv_ref[...]).astype(_BF16)
    acc = jnp.dot(s, w_ref[...].astype(_BF16), preferred_element_type=_F32)
    o_ref[...] = acc + b_ref[...]


def _ada_mods(cvec, ada_w, ada_b):
    depth, d, n = ada_w.shape
    tn = _largest_tile(n, LANES, COL_TILE_CAP)
    return pl.pallas_call(
        _ada_kernel,
        grid=(depth, n // tn),
        in_specs=[pl.BlockSpec((8, d), lambda l, j: (0, 0)),
                  pl.BlockSpec((None, d, tn), lambda l, j: (l, 0, j)),
                  pl.BlockSpec((None, 1, tn), lambda l, j: (l, 0, j))],
        out_specs=pl.BlockSpec((None, 8, tn), lambda l, j: (l, 0, j)),
        out_shape=jax.ShapeDtypeStruct((depth, 8, n), _F32),
        compiler_params=_params(2),
        name="ada_mods",
    )(cvec, ada_w, ada_b.reshape(depth, 1, n))


def _modulate_kernel(x_ref, sh_ref, sc_ref, h_ref, *, n_sub, n_lat_groups):
    i = pl.program_id(0)
    for s in range(n_sub):
        rows = slice(s * ROW_GROUP, (s + 1) * ROW_GROUP)
        is_ctx = (i * n_sub + s) >= n_lat_groups
        sh = jnp.where(is_ctx, sh_ref[1:2, :], sh_ref[0:1, :])
        sc = jnp.where(is_ctx, sc_ref[1:2, :], sc_ref[0:1, :])
        h_ref[rows, :] = (x_ref[rows, :] * (1.0 + sc) + sh).astype(_BF16)


def _modulate(xs, seq, mods):
    rows, d = xs.shape
    tm = _largest_tile(rows, ROW_GROUP, ROW_TILE_CAP)
    return pl.pallas_call(
        functools.partial(_modulate_kernel, n_sub=tm // ROW_GROUP, n_lat_groups=seq // ROW_GROUP),
        grid=(rows // tm,),
        in_specs=[pl.BlockSpec((tm, d), lambda i: (i, 0)),
                  pl.BlockSpec((None, 8, d), lambda i: (0, 0, 0)),
                  pl.BlockSpec((None, 8, d), lambda i: (0, 0, 1))],
        out_specs=pl.BlockSpec((tm, d), lambda i: (i, 0)),
        out_shape=jax.ShapeDtypeStruct((rows, d), _BF16),
        compiler_params=_params(1),
        name="modulate",
    )(xs, mods, mods)


def _rope(y, cos, sin_signed):
    lane = lax.broadcasted_iota(jnp.int32, y.shape, 1)
    partner = jnp.where((lane % 64) < 32, pltpu.roll(y, 96, 1), pltpu.roll(y, 32, 1))
    return y * cos + partner * sin_signed


def _lagged_steps(i, n_tiles, produce, consume):
    @pl.when(jnp.logical_and(i < n_tiles, i % 2 == 0))
    def _():
        consume(1)
        produce(0)

    @pl.when(jnp.logical_and(i < n_tiles, i % 2 == 1))
    def _():
        consume(0)
        produce(1)

    @pl.when(i == n_tiles)
    def _():
        consume((n_tiles - 1) % 2)


def _lagged_index_maps(n_tiles):
    return (lambda i: jnp.minimum(i, n_tiles - 1)), (lambda i: jnp.maximum(i - 1, 0))


def _head_proj_kernel(*refs, n_tiles, rope, transpose_out, has_norm, out_scale):
    it = iter(refs)
    h_ref, w_ref = next(it), next(it)
    gain_ref = next(it) if has_norm else None
    cos_ref = next(it) if rope else None
    sin_ref = next(it) if rope else None
    o_ref, wbf_ref, raw_ref = next(it), next(it), next(it)
    i = pl.program_id(1)

    @pl.when(i == 0)
    def _():
        wbf_ref[...] = w_ref[...].astype(_BF16)
        raw_ref[1] = jnp.zeros(raw_ref.shape[1:], _F32)

    def produce(slot):
        raw_ref[slot] = jnp.dot(h_ref[...], wbf_ref[...], preferred_element_type=_F32)

    def consume(slot):
        for hh in range(raw_ref.shape[2] // HEAD_DIM):
            cols = slice(hh * HEAD_DIM, (hh + 1) * HEAD_DIM)
            y = raw_ref[slot, :, cols]
            if has_norm:
                ms = jnp.mean(y * y, axis=-1, keepdims=True)
                y = y * lax.rsqrt(ms + EPS) * gain_ref[...]
            if rope:
                y = _rope(y, cos_ref[...], sin_ref[...])
            if out_scale != 1.0:
                y = y * out_scale
            if transpose_out:
                o_ref[cols, :] = y.T.astype(_BF16)
            else:
                o_ref[:, cols] = y.astype(_BF16)

    _lagged_steps(i, n_tiles, produce, consume)


def _head_proj(h, w, w_layer, col0, n_cols, *, transpose_out, rope_tabs=None, gain=None, out_scale=1.0):
    rows, d = h.shape
    tm = _largest_tile(rows, ROW_GROUP, ROW_TILE_CAP)
    tn = _largest_tile(math.gcd(n_cols, col0) if col0 else n_cols, HEAD_DIM, COL_TILE_CAP)
    j0 = col0 // tn
    n_tiles = rows // tm
    produced, consumed = _lagged_index_maps(n_tiles)
    in_specs = [pl.BlockSpec((tm, d), lambda j, i: (produced(i), 0)),
                pl.BlockSpec((None, d, tn), lambda j, i: (w_layer, 0, j0 + j))]
    args = [h, w]
    if gain is not None:
        in_specs.append(pl.BlockSpec((None, 1, HEAD_DIM), lambda j, i: (w_layer, 0, 0)))
        args.append(gain)
    if rope_tabs is not None:
        in_specs += [pl.BlockSpec((tm, HEAD_DIM), lambda j, i: (consumed(i), 0))] * 2
        args += list(rope_tabs)
    if transpose_out:
        out_spec = pl.BlockSpec((tn, tm), lambda j, i: (j, consumed(i)))
        out_shape = jax.ShapeDtypeStruct((n_cols, rows), _BF16)
    else:
        out_spec = pl.BlockSpec((tm, tn), lambda j, i: (consumed(i), j))
        out_shape = jax.ShapeDtypeStruct((rows, n_cols), _BF16)
    return pl.pallas_call(
        functools.partial(_head_proj_kernel, n_tiles=n_tiles, rope=rope_tabs is not None,
                          transpose_out=transpose_out, has_norm=gain is not None, out_scale=out_scale),
        grid=(n_cols // tn, n_tiles + 1),
        in_specs=in_specs, out_specs=out_spec, out_shape=out_shape,
        scratch_shapes=[pltpu.VMEM((d, tn), _BF16), pltpu.VMEM((2, tm, tn), _F32)],
        compiler_params=_params(2),
        name="head_proj",
    )(*args)


def _cast_kernel(w_ref, o_ref):
    o_ref[...] = w_ref[...].astype(_BF16)


def _weight_bf16(w, w_layer):
    _, k, n = w.shape
    tk = _largest_tile(k, BF16_SUBLANES, CAST_ROW_TILE_CAP)
    return pl.pallas_call(
        _cast_kernel,
        grid=(k // tk,),
        in_specs=[pl.BlockSpec((None, tk, n), lambda r: (w_layer, r, 0))],
        out_specs=pl.BlockSpec((tk, n), lambda r: (r, 0)),
        out_shape=jax.ShapeDtypeStruct((k, n), _BF16),
        compiler_params=_params(1),
        name="weight_bf16",
    )(w)


def _side_cast_fits(k_rows, n_rows):
    return k_rows // _largest_tile(k_rows, BF16_SUBLANES, CAST_ROW_TILE_CAP) <= n_rows // ROW_GROUP


def _resid_ln_kernel(*refs, n_tiles, alpha, has_bias, emit_h, n_cast, n_lat_groups):
    it = iter(refs)
    a_ref, w_ref = next(it), next(it)
    b_ref = next(it) if has_bias else None
    x_ref, gate_ref, g_ref, be_ref = next(it), next(it), next(it), next(it)
    sh_ref = next(it) if emit_h else None
    sc_ref = next(it) if emit_h else None
    cast_in_ref = next(it) if n_cast else None
    xo_ref = next(it)
    h_ref = next(it) if emit_h else None
    cast_out_ref = next(it) if n_cast else None
    raw_ref = next(it)
    i = pl.program_id(0)
    tm = raw_ref.shape[1]

    @pl.when(i == 0)
    def _():
        raw_ref[1] = jnp.zeros(raw_ref.shape[1:], _F32)

    if n_cast:
        @pl.when(i < n_cast)
        def _():
            cast_out_ref[...] = cast_in_ref[...].astype(_BF16)

    def produce(slot):
        raw_ref[slot] = jnp.dot(a_ref[...], w_ref[...], preferred_element_type=_F32)

    def consume(slot):
        is_ctx = (i - 1) >= n_lat_groups
        gate = jnp.where(is_ctx, gate_ref[1:2, :], gate_ref[0:1, :])
        if emit_h:
            sh = jnp.where(is_ctx, sh_ref[1:2, :], sh_ref[0:1, :])
            sc1 = 1.0 + jnp.where(is_ctx, sc_ref[1:2, :], sc_ref[0:1, :])
        for r0 in range(0, tm, LN_ROW_CHUNK):
            rows = slice(r0, r0 + LN_ROW_CHUNK)
            acc = raw_ref[slot, rows, :]
            if has_bias:
                acc = acc + b_ref[...]
            z = alpha * x_ref[rows, :] + gate * acc
            mu = jnp.mean(z, axis=-1, keepdims=True)
            zc = z - mu
            var = jnp.mean(zc * zc, axis=-1, keepdims=True)
            x = zc * lax.rsqrt(var + EPS) * g_ref[...] + be_ref[...]
            xo_ref[rows, :] = x
            if emit_h:
                h_ref[rows, :] = (x * sc1 + sh).astype(_BF16)

    _lagged_steps(i, n_tiles, produce, consume)


def _resid_ln(a, w_bf, bias, bias_layer, xs, n_rows, seq, mods, mod_layer, gate_idx, ln_g2, ln_b2, ln_idx,
              next_layer, next_shift_idx, alpha, *, emit_h, cast_src=None, cast_layer=0):
    k, d = w_bf.shape
    tm = ROW_GROUP
    n_tiles = n_rows // tm
    produced, consumed = _lagged_index_maps(n_tiles)
    n_cast = 0
    if cast_src is not None:
        _, kc, dc = cast_src.shape
        tkc = _largest_tile(kc, BF16_SUBLANES, CAST_ROW_TILE_CAP)
        n_cast = kc // tkc
        assert _side_cast_fits(kc, n_rows), (n_cast, n_tiles)
    in_specs = [pl.BlockSpec((tm, k), lambda i: (produced(i), 0)),
                pl.BlockSpec((k, d), lambda i: (0, 0), pipeline_mode=pl.Buffered(1))]
    args = [a, w_bf]
    if bias is not None:
        in_specs.append(pl.BlockSpec((None, 1, d), lambda i: (bias_layer, 0, 0)))
        args.append(bias)
    in_specs += [pl.BlockSpec((tm, d), lambda i: (consumed(i), 0)),
                 pl.BlockSpec((None, 8, d), lambda i: (mod_layer, 0, gate_idx)),
                 pl.BlockSpec((None, 1, d), lambda i: (ln_idx, 0, 0)),
                 pl.BlockSpec((None, 1, d), lambda i: (ln_idx, 0, 0))]
    args += [xs, mods, ln_g2, ln_b2]
    out_specs = [pl.BlockSpec((tm, d), lambda i: (consumed(i), 0))]
    out_shape = [jax.ShapeDtypeStruct((n_rows, d), _F32)]
    if emit_h:
        in_specs += [pl.BlockSpec((None, 8, d), lambda i: (next_layer, 0, next_shift_idx)),
                     pl.BlockSpec((None, 8, d), lambda i: (next_layer, 0, next_shift_idx + 1))]
        args += [mods, mods]
        out_specs.append(pl.BlockSpec((tm, d), lambda i: (consumed(i), 0)))
        out_shape.append(jax.ShapeDtypeStruct((n_rows, d), _BF16))
    if n_cast:
        in_specs.append(pl.BlockSpec((None, tkc, dc), lambda i: (cast_layer, jnp.minimum(i, n_cast - 1), 0)))
        args.append(cast_src)
        out_specs.append(pl.BlockSpec((tkc, dc), lambda i: (jnp.minimum(i, n_cast - 1), 0)))
        out_shape.append(jax.ShapeDtypeStruct((kc, dc), _BF16))
    return pl.pallas_call(
        functools.partial(_resid_ln_kernel, n_tiles=n_tiles, alpha=alpha, has_bias=bias is not None,
                          emit_h=emit_h, n_cast=n_cast, n_lat_groups=seq // ROW_GROUP),
        grid=(n_tiles + 1,),
        in_specs=in_specs, out_specs=out_specs, out_shape=out_shape,
        scratch_shapes=[pltpu.VMEM((2, tm, d), _F32)],
        compiler_params=_params(1),
        name="resid_ln",
    )(*args)


def _ffn_up_kernel(hp_ref, hm_ref, hn_ref, wv_ref, wg_ref, cwv_ref, cwg_ref, cbv_ref, cbg_ref, o_ref,
                   wv_bf, wg_bf, hs_ref, *, seq, rows):
    i = pl.program_id(1)
    tm = hm_ref.shape[0]
    halo = BF16_SUBLANES

    @pl.when(i == 0)
    def _():
        wv_bf[...] = wv_ref[...].astype(_BF16)
        wg_bf[...] = wg_ref[...].astype(_BF16)

    hs_ref[0:halo, :] = hp_ref[...]
    hs_ref[halo:halo + tm, :] = hm_ref[...]
    hs_ref[halo + tm:, :] = hn_ref[...]
    hs = hs_ref[...]
    r = i * tm + lax.broadcasted_iota(jnp.int32, (tm, 1), 0)
    keep_prev = jnp.where(r == 0, 0.0, 1.0) * jnp.where(r == seq, 0.0, 1.0)
    keep_next = jnp.where(r == seq - 1, 0.0, 1.0) * jnp.where(r == rows - 1, 0.0, 1.0)

    def conv(u, cw_ref, cb_ref):
        cw = cw_ref[...]
        return (cw[1:2, :] * u[halo:halo + tm, :]
                + keep_prev * (cw[0:1, :] * u[halo - 1:halo - 1 + tm, :])
                + keep_next * (cw[2:3, :] * u[halo + 1:halo + 1 + tm, :])
                + cb_ref[...])

    u_gate = jnp.dot(hs, wg_bf[...], preferred_element_type=_F32)
    u_val = jnp.dot(hs, wv_bf[...], preferred_element_type=_F32)
    gate = _silu(conv(u_gate, cwg_ref, cbg_ref))
    o_ref[...] = (gate * conv(u_val, cwv_ref, cbv_ref)).astype(_BF16)


def _ffn_up(h, w_up, conv_w, conv_b3, layer, seq):
    rows, d = h.shape
    d_ff = w_up.shape[2] // 2
    tm = _largest_tile(rows, ROW_GROUP, ROW_TILE_CAP)
    tf = _largest_tile(d_ff, LANES, FFN_COL_TILE_CAP)
    nf = d_ff // tf
    halo = BF16_SUBLANES
    per_tile = tm // halo
    last_halo_blk = rows // halo - 1
    return pl.pallas_call(
        functools.partial(_ffn_up_kernel, seq=seq, rows=rows),
        grid=(nf, rows // tm),
        in_specs=[pl.BlockSpec((halo, d), lambda j, i: (jnp.maximum(i * per_tile - 1, 0), 0)),
                  pl.BlockSpec((tm, d), lambda j, i: (i, 0)),
                  pl.BlockSpec((halo, d), lambda j, i: (jnp.minimum((i + 1) * per_tile, last_halo_blk), 0)),
                  pl.BlockSpec((None, d, tf), lambda j, i: (layer, 0, j)),
                  pl.BlockSpec((None, d, tf), lambda j, i: (layer, 0, nf + j)),
                  pl.BlockSpec((None, 3, tf), lambda j, i: (layer, 0, j)),
                  pl.BlockSpec((None, 3, tf), lambda j, i: (layer, 0, nf + j)),
                  pl.BlockSpec((None, 1, tf), lambda j, i: (layer, 0, j)),
                  pl.BlockSpec((None, 1, tf), lambda j, i: (layer, 0, nf + j))],
        out_specs=pl.BlockSpec((tm, tf), lambda j, i: (i, j)),
        out_shape=jax.ShapeDtypeStruct((rows, d_ff), _BF16),
        scratch_shapes=[pltpu.VMEM((d, tf), _BF16), pltpu.VMEM((d, tf), _BF16),
                        pltpu.VMEM((tm + 2 * halo, d), _BF16)],
        compiler_params=_params(2),
        name="ffn_up",
    )(h, h, h, w_up, w_up, conv_w, conv_w, conv_b3, conv_b3)


_NEG_BIG = -1e30
_LOG2E = math.log2(math.e)
ATTN_KEY_CHUNK_CAP = 1024


def _key_chunk(seq, ctx):
    assert ctx % 128 == 0, ctx
    return _largest_tile(seq, LANES, ATTN_KEY_CHUNK_CAP)


def _attend_t(qt_list, k_slices, vt_slice, acc_refs, s_ref, sc_ref, *, latent, seq, ctx, tk):
    n_maps = len(qt_list)
    tq = qt_list[0].shape[1]

    def scores(rows, c):
        return jnp.dot(k_slices[c](rows), qt_list[c], preferred_element_type=_F32)

    def absorb(s, vt, c, m, l):
        m_new = jnp.maximum(m, jnp.max(s, axis=0, keepdims=True))
        p = jnp.exp2(s - m_new)
        alpha = jnp.exp2(m - m_new)
        acc_refs[c][...] = alpha * acc_refs[c][...] + jnp.dot(vt, p.astype(_BF16), preferred_element_type=_F32)
        return m_new, alpha * l + jnp.sum(p, axis=0, keepdims=True)

    for acc_ref in acc_refs:
        acc_ref[...] = jnp.zeros(acc_ref.shape, _F32)
    carry = tuple((jnp.full((1, tq), _NEG_BIG, _F32), jnp.zeros((1, tq), _F32)) for _ in range(n_maps))
    ctx_rows = slice(seq, seq + ctx)
    n_chunks = seq // tk if latent else 0
    for c in range(n_maps):
        if latent:
            s_ref[0, c] = scores(slice(0, tk), c)
        else:
            sc_ref[c] = scores(ctx_rows, c)
    for n in range(n_chunks):
        for c in range(n_maps):
            if n + 1 < n_chunks:
                s_ref[(n + 1) % 2, c] = scores(slice((n + 1) * tk, (n + 2) * tk), c)
            else:
                sc_ref[c] = scores(ctx_rows, c)
        vt = vt_slice(slice(n * tk, (n + 1) * tk))
        carry = tuple(absorb(s_ref[n % 2, c], vt, c, *carry[c]) for c in range(n_maps))
    vt_ctx = vt_slice(ctx_rows)
    carry = tuple(absorb(sc_ref[c], vt_ctx, c, *carry[c]) for c in range(n_maps))
    return [l for (_, l) in carry]


def _per_query_block(i, n_latent_blocks, run):
    @pl.when(i < n_latent_blocks)
    def _():
        run(True)

    @pl.when(i >= n_latent_blocks)
    def _():
        run(False)


def _gqa_attn_kernel(qt_ref, k_ref, vt_ref, o_ref, s_ref, sc_ref, *acc_refs, seq, ctx, tk):
    tq = qt_ref.shape[1]

    def run(latent):
        qt_list = [qt_ref[g * HEAD_DIM:(g + 1) * HEAD_DIM, :] for g in range(GQA_GROUP)]
        k_slices = [lambda rows: k_ref[rows, :]] * GQA_GROUP
        sums = _attend_t(qt_list, k_slices, lambda rows: vt_ref[:, rows], acc_refs, s_ref, sc_ref,
                         latent=latent, seq=seq, ctx=ctx, tk=tk)
        for g in range(GQA_GROUP):
            o_t = acc_refs[g][...] * (1.0 / sums[g])
            o_ref[:, g * HEAD_DIM:(g + 1) * HEAD_DIM] = o_t.T.astype(_BF16)

    _per_query_block(pl.program_id(1), seq // tq, run)


def _gqa_attention(q_t, k, v_t, seq, ctx):
    dq, rows = q_t.shape
    kvh = k.shape[1] // HEAD_DIM
    tq = ROW_GROUP
    qw = GQA_GROUP * HEAD_DIM
    tk = _key_chunk(seq, ctx)
    return pl.pallas_call(
        functools.partial(_gqa_attn_kernel, seq=seq, ctx=ctx, tk=tk),
        grid=(kvh, rows // tq),
        in_specs=[pl.BlockSpec((qw, tq), lambda g, i: (g, i)),
                  pl.BlockSpec((rows, HEAD_DIM), lambda g, i: (0, g)),
                  pl.BlockSpec((HEAD_DIM, rows), lambda g, i: (g, 0))],
        out_specs=pl.BlockSpec((tq, qw), lambda g, i: (i, g)),
        out_shape=jax.ShapeDtypeStruct((rows, dq), _BF16),
        scratch_shapes=[pltpu.VMEM((2, GQA_GROUP, tk, tq), _F32), pltpu.VMEM((GQA_GROUP, ctx, tq), _F32)]
        + [pltpu.VMEM((HEAD_DIM, tq), _F32)] * GQA_GROUP,
        compiler_params=_params(2),
        name="gqa_attention",
    )(q_t, k, v_t)


def _diff_attn_kernel(qt_ref, k_ref, vt_ref, lam_ref, sub_ref, o_ref, s_ref, sc_ref, *acc_refs,
                      seq, ctx, tk, lambda_init):
    tq = qt_ref.shape[1]

    def run(latent):
        qt_list = [qt_ref[c * HEAD_DIM:(c + 1) * HEAD_DIM, :] for c in range(2)]
        k_slices = [functools.partial(lambda rows, c: k_ref[rows, c * HEAD_DIM:(c + 1) * HEAD_DIM], c=c)
                    for c in range(2)]
        l0, l1 = _attend_t(qt_list, k_slices, lambda rows: vt_ref[:, rows], acc_refs, s_ref, sc_ref,
                           latent=latent, seq=seq, ctx=ctx, tk=tk)
        lp = lam_ref[...]
        lam = (jnp.exp(jnp.sum(lp[0:1, :] * lp[1:2, :], axis=1, keepdims=True))
               - jnp.exp(jnp.sum(lp[2:3, :] * lp[3:4, :], axis=1, keepdims=True)) + lambda_init)
        o_t = acc_refs[0][...] * (1.0 / l0) - lam * (acc_refs[1][...] * (1.0 / l1))
        o = o_t.T
        ms = jnp.mean(o * o, axis=-1, keepdims=True)
        o = o * lax.rsqrt(ms + EPS) * sub_ref[...] * (1.0 - lambda_init)
        o_ref[...] = o.astype(_BF16)

    _per_query_block(pl.program_id(1), seq // tq, run)


def _diff_attention(q_t, k, v_t, lam_params, subln, layer_j, seq, ctx, lambda_init):
    d, rows = q_t.shape
    hw = 2 * HEAD_DIM
    tq = ROW_GROUP
    tk = _key_chunk(seq, ctx)
    return pl.pallas_call(
        functools.partial(_diff_attn_kernel, seq=seq, ctx=ctx, tk=tk, lambda_init=lambda_init),
        grid=(d // hw, rows // tq),
        in_specs=[pl.BlockSpec((hw, tq), lambda h, i: (h, i)),
                  pl.BlockSpec((rows, hw), lambda h, i: (0, h)),
                  pl.BlockSpec((hw, rows), lambda h, i: (h, 0)),
                  pl.BlockSpec((None, 4, HEAD_DIM), lambda h, i: (layer_j, 0, 0)),
                  pl.BlockSpec((None, 1, hw), lambda h, i: (layer_j, 0, 0))],
        out_specs=pl.BlockSpec((tq, hw), lambda h, i: (i, h)),
        out_shape=jax.ShapeDtypeStruct((rows, d), _BF16),
        scratch_shapes=[pltpu.VMEM((2, 2, tk, tq), _F32), pltpu.VMEM((2, ctx, tq), _F32)]
        + [pltpu.VMEM((hw, tq), _F32)] * 2,
        compiler_params=_params(2),
        name="diff_attention",
    )(q_t, k, v_t, lam_params, subln)


def _dft_cos_sin(n):
    kt = np.outer(np.arange(n), np.arange(n)) % n
    ang = 2.0 * np.pi * kt / n
    return np.cos(ang), np.sin(ang)


def _channel_dft(z, cs, scale, store):
    p = z.shape[0] // 2
    gd = cs.shape[1]
    zb = z.astype(_BF16)
    for g in range(z.shape[1] // gd):
        cols = slice(g * gd, (g + 1) * gd)
        zc = jnp.concatenate([zb[:p, cols], zb[p:, cols]], axis=1)
        y = jnp.dot(zc, cs, preferred_element_type=_F32) * scale
        store(cols, y.astype(_BF16))


def _fft_stage1_kernel(a_ref, f1_ref, tw_ref, o_ref, *, d):
    n1 = a_ref.shape[0]
    f1 = f1_ref[...].astype(_BF16)
    for tt in range(a_ref.shape[1] // d):
        cols = slice(tt * d, (tt + 1) * d)
        a = jnp.dot(f1, a_ref[:, cols], preferred_element_type=_F32)
        ar, ai = a[:n1, :], a[n1:, :]
        twr, twi = tw_ref[0, :, tt:tt + 1], tw_ref[1, :, tt:tt + 1]
        o_ref[0, :, cols] = (ar * twr - ai * twi).astype(_BF16)
        o_ref[1, :, cols] = (ar * twi + ai * twr).astype(_BF16)


def _fft_stage2_kernel(b_ref, g2_ref, cs_ref, y_ref, g2_bf, cs_bf, *, scale):
    @pl.when(pl.program_id(0) == 0)
    def _():
        g2_bf[...] = g2_ref[...].astype(_BF16)
        cs_bf[...] = cs_ref[...].astype(_BF16)

    d = b_ref.shape[3]
    for kk in range(b_ref.shape[1]):
        bcat = jnp.concatenate([b_ref[0, kk], b_ref[1, kk]], axis=0)
        z = jnp.dot(g2_bf[...], bcat, preferred_element_type=_F32)

        def store(cols, y, kk=kk):
            y_ref[:, kk * d + cols.start:kk * d + cols.stop] = y

        _channel_dft(z, cs_bf[...], scale, store)


def _ctx_dft_kernel(a_ref, fc_ref, cs_ref, y_ref, *, scale):
    z = jnp.dot(fc_ref[...].astype(_BF16), a_ref[...], preferred_element_type=_F32)

    def store(cols, y):
        y_ref[:, cols] = y

    _channel_dft(z, cs_ref[...].astype(_BF16), scale, store)


def _fourier_mix(h, seq, ctx):
    rows, d = h.shape
    gd = d // FNET_GROUPS
    n2 = FFT_MINOR
    n1 = seq // n2
    g2 = FFT_STAGE1_POSITIONS
    kb = FFT_STAGE2_FREQS
    assert seq % n2 == 0 and rows % n2 == 0 and n2 % g2 == 0 and n1 % kb == 0

    c1, s1 = _dft_cos_sin(n1)
    f1 = jnp.asarray(np.concatenate([c1, -s1], axis=0), _F32)
    ang = 2.0 * np.pi * np.outer(np.arange(n1), np.arange(n2)) / seq
    tw = np.stack([np.cos(ang), -np.sin(ang)])
    tw = jnp.asarray(tw.reshape(2, n1, n2 // g2, g2).transpose(2, 0, 1, 3), _F32)
    c2, s2 = _dft_cos_sin(n2)
    g2m = jnp.asarray(np.block([[c2, s2], [-s2, c2]]), _F32)
    cc, sc = _dft_cos_sin(gd)
    cs = jnp.asarray(np.concatenate([cc, sc], axis=0), _F32)
    cx, sx = _dft_cos_sin(ctx)
    fc = jnp.asarray(np.concatenate([cx, -sx], axis=0), _F32)

    b = pl.pallas_call(
        functools.partial(_fft_stage1_kernel, d=d),
        grid=(n2 // g2,),
        in_specs=[pl.BlockSpec((n1, g2 * d), lambda c: (0, c)),
                  pl.BlockSpec((2 * n1, n1), lambda c: (0, 0)),
                  pl.BlockSpec((None, 2, n1, g2), lambda c: (c, 0, 0, 0))],
        out_specs=pl.BlockSpec((2, n1, g2 * d), lambda c: (0, 0, c)),
        out_shape=jax.ShapeDtypeStruct((2, n1, n2 * d), _BF16),
        compiler_params=_params(1),
        name="fft_stage1",
    )(h.reshape(rows // n2, n2 * d), f1, tw)

    y_lat = pl.pallas_call(
        functools.partial(_fft_stage2_kernel, scale=1.0 / math.sqrt(seq * gd)),
        grid=(n1 // kb,),
        in_specs=[pl.BlockSpec((2, kb, n2, d), lambda k: (0, k, 0, 0)),
                  pl.BlockSpec((2 * n2, 2 * n2), lambda k: (0, 0)),
                  pl.BlockSpec((2 * gd, gd), lambda k: (0, 0))],
        out_specs=pl.BlockSpec((n2, kb * d), lambda k: (0, k)),
        out_shape=jax.ShapeDtypeStruct((n2, n1 * d), _BF16),
        scratch_shapes=[pltpu.VMEM((2 * n2, 2 * n2), _BF16), pltpu.VMEM((2 * gd, gd), _BF16)],
        compiler_params=_params(1),
        name="fft_stage2",
    )(b.reshape(2, n1, n2, d), g2m, cs)

    y_ctx = pl.pallas_call(
        functools.partial(_ctx_dft_kernel, scale=1.0 / math.sqrt(ctx * gd)),
        grid=(1,),
        in_specs=[pl.BlockSpec((ctx, d), lambda c: (seq // ctx, 0)),
                  pl.BlockSpec((2 * ctx, ctx), lambda c: (0, 0)),
                  pl.BlockSpec((2 * gd, gd), lambda c: (0, 0))],
        out_specs=pl.BlockSpec((ctx, d), lambda c: (0, 0)),
        out_shape=jax.ShapeDtypeStruct((ctx, d), _BF16),
        compiler_params=_params(1),
        name="ctx_dft",
    )(h, fc, cs)
    return jnp.concatenate([y_lat.reshape(seq, d), y_ctx], axis=0)


def _rope_tables(seq, ctx):
    quarter = HEAD_DIM // 4
    t = np.arange(seq)
    inv_freq = ROPE_THETA ** (-np.arange(quarter, dtype=np.float64) / quarter)
    ang_r = (t // GRID_W)[:, None] * inv_freq[None, :]
    ang_c = (t % GRID_W)[:, None] * inv_freq[None, :]
    cos = np.concatenate([np.cos(ang_r)] * 2 + [np.cos(ang_c)] * 2, axis=1)
    sin = np.concatenate([-np.sin(ang_r), np.sin(ang_r), -np.sin(ang_c), np.sin(ang_c)], axis=1)
    cos = np.concatenate([cos, np.ones((ctx, HEAD_DIM))], axis=0)
    sin = np.concatenate([sin, np.zeros((ctx, HEAD_DIM))], axis=0)
    return jnp.asarray(cos, _F32), jnp.asarray(sin, _F32)


def kernel(x, c, ctx, c_ctx, ada_w, ada_b, ln_g, ln_b, gqa_w_qkv, gqa_q_norm, gqa_k_norm, gqa_w_o,
           diff_w_qkv, diff_lambda, diff_subln, diff_w_o, fnet_w, fnet_b,
           ffn_w_up, ffn_conv_w, ffn_conv_b, ffn_w_down):
    batch, seq, d = x.shape
    n_ctx = ctx.shape[1]
    depth = ada_w.shape[0]
    assert batch == 1 and seq % ROW_GROUP == 0 and n_ctx % ROW_GROUP == 0 and seq % GRID_W == 0
    alpha = (2 * depth) ** 0.25
    q_scale = HEAD_DIM ** -0.5 * _LOG2E

    cvec = jnp.concatenate([c, c_ctx[None, :], jnp.zeros((6, d), _F32)], axis=0)
    mods = _ada_mods(cvec, ada_w, ada_b)
    xs = jnp.concatenate([x[0], ctx[0]], axis=0)
    rows = seq + n_ctx
    rope_tabs = _rope_tables(seq, n_ctx)
    ln_g2 = ln_g.reshape(depth * 2, 1, d)
    ln_b2 = ln_b.reshape(depth * 2, 1, d)
    conv_b3 = ffn_conv_b.reshape(depth, 1, -1)
    gqa_q_norm3 = gqa_q_norm.reshape(-1, 1, HEAD_DIM)
    gqa_k_norm3 = gqa_k_norm.reshape(-1, 1, HEAD_DIM)
    diff_subln3 = diff_subln.reshape(-1, 1, 2 * HEAD_DIM)
    fnet_b3 = fnet_b.reshape(-1, 1, d)

    h = _modulate(xs, seq, mods)
    counts = [0, 0, 0]
    for i in range(depth):
        last = i == depth - 1
        kind = i % N_MIXERS
        j = counts[kind]
        counts[kind] += 1
        if kind == 0:
            n_q = d
            n_kv = d // GQA_GROUP
            q_t = _head_proj(h, gqa_w_qkv, j, 0, n_q, transpose_out=True, rope_tabs=rope_tabs,
                             gain=gqa_q_norm3, out_scale=q_scale)
            k = _head_proj(h, gqa_w_qkv, j, n_q, n_kv, transpose_out=False, rope_tabs=rope_tabs, gain=gqa_k_norm3)
            v_t = _head_proj(h, gqa_w_qkv, j, n_q + n_kv, n_kv, transpose_out=True)
            mixed, w_mix, bias = _gqa_attention(q_t, k, v_t, seq, n_ctx), gqa_w_o, None
        elif kind == 1:
            lambda_init = 0.8 - 0.6 * math.exp(-0.3 * i)
            q_t = _head_proj(h, diff_w_qkv, j, 0, d, transpose_out=True, rope_tabs=rope_tabs, out_scale=q_scale)
            k = _head_proj(h, diff_w_qkv, j, d, d, transpose_out=False, rope_tabs=rope_tabs)
            v_t = _head_proj(h, diff_w_qkv, j, 2 * d, d, transpose_out=True)
            mixed = _diff_attention(q_t, k, v_t, diff_lambda, diff_subln3, j, seq, n_ctx, lambda_init)
            w_mix, bias = diff_w_o, None
        else:
            mixed, w_mix, bias = _fourier_mix(h, seq, n_ctx), fnet_w, fnet_b3
        if _side_cast_fits(ffn_w_down.shape[1], rows):
            xs, h, w_down = _resid_ln(mixed, _weight_bf16(w_mix, j), bias, j, xs, rows, seq, mods, i, 2, ln_g2,
                                      ln_b2, 2 * i, i, 3, alpha, emit_h=True, cast_src=ffn_w_down, cast_layer=i)
        else:
            xs, h = _resid_ln(mixed, _weight_bf16(w_mix, j), bias, j, xs, rows, seq, mods, i, 2, ln_g2, ln_b2,
                              2 * i, i, 3, alpha, emit_h=True)
            w_down = _weight_bf16(ffn_w_down, i)
        a = _ffn_up(h, ffn_w_up, ffn_conv_w, conv_b3, i, seq)
        if last:
            (xs,) = _resid_ln(a, w_down, None, 0, xs, seq, seq, mods, i, 5, ln_g2, ln_b2, 2 * i + 1,
                              0, 0, alpha, emit_h=False)
        else:
            xs, h = _resid_ln(a, w_down, None, 0, xs, rows, seq, mods, i, 5, ln_g2, ln_b2, 2 * i + 1,
                              i + 1, 0, alpha, emit_h=True)
    return xs[None]
```
